```python
import jax, jax.numpy as jnp
from jax import lax
import numpy as np

D_MODEL = 1024
BATCH = 8
SEQ = 2048
DEPTH = 2
DEC_BATCH = 128
DEC_SEQ = 1
PAST_LEN = 16384
PAGE_SIZE = 128

HEAD_DIM = 64
D_MIX = D_MODEL
D_A = 3 * D_MIX // 8
D_C = 3 * D_MIX // 8
D_B = D_MIX - D_A - D_C
POOL_WINDOWS = (2, 4, 8, 16)
N_POOL_GROUPS = len(POOL_WINDOWS)
POOL_GROUP_DIM = D_B // N_POOL_GROUPS
POOL_STATE = max(POOL_WINDOWS) - 1
CONV_A_WIDTH = 31
CONV_C_WIDTH = 3
D_IN = 2 * D_A + D_B + 3 * D_C
D_FF = 4 * D_MODEL
D_PLE = 256
EPS = 1e-6

kernel_name = "hybrid_conv_pool_shortconv_decoder_step"


def rms_norm(x, g):
    x32 = x.astype(jnp.float32)
    y = x32 * lax.rsqrt(jnp.mean(x32 * x32, axis=-1, keepdims=True) + EPS)
    return (y * g.astype(jnp.float32)).astype(x.dtype)


def layer_norm(x, g, b):
    x32 = x.astype(jnp.float32)
    mu = jnp.mean(x32, axis=-1, keepdims=True)
    xc = x32 - mu
    var = jnp.mean(xc * xc, axis=-1, keepdims=True)
    y = xc * lax.rsqrt(var + EPS)
    return (y * g.astype(jnp.float32) + b.astype(jnp.float32)).astype(x.dtype)


def causal_depthwise_conv(xe, w):
    c = xe.shape[-1]
    rhs = w.astype(xe.dtype)[:, None, :]
    return lax.conv_general_dilated(xe, rhs, window_strides=(1,), padding="VALID",
                                    dimension_numbers=("NWC", "WIO", "NWC"),
                                    feature_group_count=c)


def multiscale_pool(ue, pos0):
    t_new = ue.shape[1] - POOL_STATE
    u32 = ue.astype(jnp.float32)
    cs = jnp.pad(jnp.cumsum(u32, axis=1), ((0, 0), (1, 0), (0, 0)))
    end = cs[:, POOL_STATE + 1:]
    pos = pos0 + jnp.arange(t_new, dtype=jnp.int32)
    parts = []
    for g, w in enumerate(POOL_WINDOWS):
        sl = slice(g * POOL_GROUP_DIM, (g + 1) * POOL_GROUP_DIM)
        start = cs[:, POOL_STATE + 1 - w: POOL_STATE + 1 - w + t_new, sl]
        count = jnp.minimum(pos + 1, w).astype(jnp.float32)[None, :, None]
        parts.append((end[..., sl] - start) / count)
    pooled = jnp.concatenate(parts, axis=-1)
    return (pooled - u32[:, POOL_STATE:]).astype(ue.dtype)


def trunk_layer(h, p_i, st_a, st_p, st_c, pos0,
                g_mix, w_in, conv_a_w, conv_a_b, ln_a_g, ln_a_b, pool_w, pool_scale,
                conv_c_w, w_out, g_mlp, w_up, w_down, g_ple, w_ple_gate, w_ple_proj):
    bsz, t_new, _ = h.shape
    n = rms_norm(h, g_mix)
    z = n @ w_in
    a_v, a_g, u_b, c_b, c_c, c_x = jnp.split(
        z, np.cumsum([D_A, D_A, D_B, D_C, D_C]).tolist(), axis=-1)

    glu = a_v * jax.nn.sigmoid(a_g)
    ge = jnp.concatenate([st_a, glu], axis=1)
    ya = causal_depthwise_conv(ge, conv_a_w) + conv_a_b
    out_a = jax.nn.silu(layer_norm(ya, ln_a_g, ln_a_b))
    new_a = ge[:, -(CONV_A_WIDTH - 1):]

    ue = jnp.concatenate([st_p, u_b], axis=1)
    pooled = multiscale_pool(ue, pos0).reshape(bsz, t_new, N_POOL_GROUPS, POOL_GROUP_DIM)
    out_b = jnp.einsum("btgc,gcd->btgd", pooled, pool_w).reshape(bsz, t_new, D_B) * pool_scale
    new_p = ue[:, -POOL_STATE:]

    ve = jnp.concatenate([st_c, c_c * c_x], axis=1)
    out_c = c_b * causal_depthwise_conv(ve, conv_c_w)
    new_c = ve[:, -(CONV_C_WIDTH - 1):]

    h = h + jnp.concatenate([out_a, out_b, out_c], axis=-1) @ w_out

    m = rms_norm(h, g_mlp)
    h = h + jnp.square(jax.nn.relu(m @ w_up)) @ w_down

    gate = jax.nn.sigmoid(rms_norm(h, g_ple) @ w_ple_gate)
    h = h + gate * (p_i @ w_ple_proj)
    return h, new_a, new_p, new_c


def setup_inputs(seed: int = 0) -> dict:
    key = jax.random.key(seed)
    ks = jax.random.split(key, 32)
    f32 = jnp.float32
    nrm = lambda k, shape, s: jax.random.normal(k, shape, f32) * s
    return {
        "x_prompt": nrm(ks[0], (BATCH, SEQ, D_MODEL), 1.0),
        "x_sample": nrm(ks[1], (DEC_BATCH, DEC_SEQ, D_MODEL), 1.0),
        "state_conv_a": nrm(ks[2], (DEPTH, DEC_BATCH, CONV_A_WIDTH - 1, D_A), 1.0),
        "state_pool": nrm(ks[3], (DEPTH, DEC_BATCH, POOL_STATE, D_B), 1.0),
        "state_conv_c": nrm(ks[4], (DEPTH, DEC_BATCH, CONV_C_WIDTH - 1, D_C), 1.0),
        "p_prompt": nrm(ks[5], (DEPTH, BATCH, SEQ, D_PLE), 1.0),
        "p_sample": nrm(ks[6], (DEPTH, DEC_BATCH, DEC_SEQ, D_PLE), 1.0),
        "norm_mix_g": 1.0 + nrm(ks[7], (DEPTH, D_MODEL), 0.05),
        "w_in": nrm(ks[8], (DEPTH, D_MODEL, D_IN), D_MODEL ** -0.5),
        "conv_a_w": nrm(ks[9], (DEPTH, CONV_A_WIDTH, D_A), CONV_A_WIDTH ** -0.5),
        "conv_a_b": nrm(ks[10], (DEPTH, D_A), 0.02),
        "ln_a_g": 1.0 + nrm(ks[11], (DEPTH, D_A), 0.05),
        "ln_a_b": nrm(ks[12], (DEPTH, D_A), 0.02),
        "pool_w": nrm(ks[13], (DEPTH, N_POOL_GROUPS, POOL_GROUP_DIM, POOL_GROUP_DIM), POOL_GROUP_DIM ** -0.5),
        "pool_scale": 1.0 + nrm(ks[14], (DEPTH, D_B), 0.1),
        "conv_c_w": nrm(ks[15], (DEPTH, CONV_C_WIDTH, D_C), CONV_C_WIDTH ** -0.5),
        "w_out": nrm(ks[16], (DEPTH, D_MIX, D_MODEL), D_MIX ** -0.5),
        "norm_mlp_g": 1.0 + nrm(ks[17], (DEPTH, D_MODEL), 0.05),
        "w_up": nrm(ks[18], (DEPTH, D_MODEL, D_FF), D_MODEL ** -0.5),
        "w_down": nrm(ks[19], (DEPTH, D_FF, D_MODEL), D_FF ** -0.5),
        "norm_ple_g": 1.0 + nrm(ks[20], (DEPTH, D_MODEL), 0.05),
        "w_ple_gate": nrm(ks[21], (DEPTH, D_MODEL, D_MODEL), D_MODEL ** -0.5),
        "w_ple_proj": nrm(ks[22], (DEPTH, D_PLE, D_MODEL), D_PLE ** -0.5),
        "final_norm_g": 1.0 + nrm(ks[23], (D_MODEL,), 0.05),
    }


def reference(x_prompt, x_sample, state_conv_a, state_pool, state_conv_c, p_prompt, p_sample,
              norm_mix_g, w_in, conv_a_w, conv_a_b, ln_a_g, ln_a_b, pool_w, pool_scale, conv_c_w,
              w_out, norm_mlp_g, w_up, w_down, norm_ple_g, w_ple_gate, w_ple_proj, final_norm_g):
    bp = x_prompt.shape[0]
    dt = x_prompt.dtype
    zero_a = jnp.zeros((bp, CONV_A_WIDTH - 1, D_A), dt)
    zero_p = jnp.zeros((bp, POOL_STATE, D_B), dt)
    zero_c = jnp.zeros((bp, CONV_C_WIDTH - 1, D_C), dt)
    hp, hs = x_prompt, x_sample
    pa, pp, pc, sa, sp, sc = [], [], [], [], [], []
    for i in range(DEPTH):
        lw = (norm_mix_g[i], w_in[i], conv_a_w[i], conv_a_b[i], ln_a_g[i], ln_a_b[i], pool_w[i],
              pool_scale[i], conv_c_w[i], w_out[i], norm_mlp_g[i], w_up[i], w_down[i],
              norm_ple_g[i], w_ple_gate[i], w_ple_proj[i])
        hp, a_i, p_i, c_i = trunk_layer(hp, p_prompt[i], zero_a, zero_p, zero_c, 0, *lw)
        hs, a_j, p_j, c_j = trunk_layer(hs, p_sample[i], state_conv_a[i], state_pool[i],
                                        state_conv_c[i], PAST_LEN, *lw)
        pa.append(a_i); pp.append(p_i); pc.append(c_i)
        sa.append(a_j); sp.append(p_j); sc.append(c_j)
    y_prompt = rms_norm(hp, final_norm_g)
    y_sample = rms_norm(hs, final_norm_g)
    new_conv_a_prompt = jnp.stack(pa)
    new_pool_prompt = jnp.stack(pp)
    new_conv_c_prompt = jnp.stack(pc)
    new_conv_a_sample = jnp.stack(sa)
    new_pool_sample = jnp.stack(sp)
    new_conv_c_sample = jnp.stack(sc)
    return (y_prompt, y_sample, new_conv_a_prompt, new_pool_prompt, new_conv_c_prompt,
            new_conv_a_sample, new_pool_sample, new_conv_c_sample)
```

```python
import functools

import jax
import jax.numpy as jnp
from jax import lax
from jax.experimental import pallas as pl
from jax.experimental.pallas import tpu as pltpu

D_MODEL = 1024
D_A = 384
D_B = 256
D_C = 384
D_IN = 2 * D_A + D_B + 3 * D_C
D_FF = 4096
D_PLE = 256
POOL_WINDOWS = (2, 4, 8, 16)
POOL_GROUP_DIM = D_B // len(POOL_WINDOWS)
POOL_STATE = 15
CONV_A_WIDTH = 31
CONV_C_WIDTH = 3
PAST_LEN = 16384
EPS = 1e-6

_O_AV = 0
_O_AG = D_A
_O_UB = 2 * D_A
_O_CB = _O_UB + D_B
_O_CC = _O_CB + D_C
_O_CX = _O_CC + D_C
_M_A = 0
_M_B = D_A
_M_C = D_A + D_B

LANES = 128
HIST_A = 32
HIST_P = 16
HIST_C = 8
ROW_CHUNK = 32
FF_CHUNK = 1024
SEQ_TILE = 512
VMEM_LIMIT_BYTES = 60 * 1024 * 1024

BF16 = jnp.bfloat16
F32 = jnp.float32


def _rms_norm(x, g):
    ms = jnp.mean(x * x, axis=-1, keepdims=True)
    return x * lax.rsqrt(ms + EPS) * g


def _dot(a, b):
    return jnp.dot(a, b, preferred_element_type=F32)


def _layer_norm_silu(ya, g, b):
    mu = jnp.mean(ya, axis=-1, keepdims=True)
    xc = ya - mu
    var = jnp.mean(xc * xc, axis=-1, keepdims=True)
    y = xc * lax.rsqrt(var + EPS) * g + b
    return y * jax.nn.sigmoid(y)


def _pool_select(s_small, s_big, w_small, w_big, pos1):
    lane = lax.broadcasted_iota(jnp.int32, s_small.shape, 1)
    cnt_small = jnp.minimum(pos1, w_small).astype(F32)
    cnt_big = jnp.minimum(pos1, w_big).astype(F32)
    return jnp.where(lane < POOL_GROUP_DIM, s_small / cnt_small, s_big / cnt_big)


def _channel_mix_tail(h, mix_bf16, p, w_out, g_mlp, w_up, w_down, g_ple, w_pg, w_pp, g_fin, final):
    h = h + _dot(mix_bf16, w_out[...])
    m = _rms_norm(h, g_mlp[...]).astype(BF16)
    acc = None
    for c in range(D_FF // FF_CHUNK):
        cols = slice(c * FF_CHUNK, (c + 1) * FF_CHUNK)
        up = _dot(m, w_up[:, cols])
        act = jnp.square(jnp.maximum(up, 0.0)).astype(BF16)
        down = _dot(act, w_down[cols, :])
        acc = down if acc is None else acc + down
    h = h + acc
    r = _rms_norm(h, g_ple[...]).astype(BF16)
    gate = jax.nn.sigmoid(_dot(r, w_pg[...]))
    h = h + gate * _dot(p.astype(BF16), w_pp[...])
    if final:
        h = _rms_norm(h, g_fin[...])
    return h


def _prompt_kernel(h_ref, p_ref, g_mix, w_in, caw, cab, lng, lnb, pool_bd, pscale, ccw,
                   w_out, g_mlp, w_up, w_down, g_ple, w_pg, w_pp, g_fin,
                   out_ref, na_ref, np_ref, nc_ref,
                   zbuf, gbuf, ubuf, vbuf, poolbuf, mixbuf, *, tile, final):
    t = pl.program_id(1)
    n_t = pl.num_programs(1)

    @pl.when(t == 0)
    def _():
        gbuf[0:HIST_A, :] = jnp.zeros((HIST_A, D_A), F32)
        ubuf[0:HIST_P, :] = jnp.zeros((HIST_P, D_B), F32)
        vbuf[0:HIST_C, :] = jnp.zeros((HIST_C, D_C), F32)

    h = h_ref[0]
    n = _rms_norm(h, g_mix[...]).astype(BF16)
    zbuf[...] = _dot(n, w_in[...])

    n_chunks = tile // ROW_CHUNK

    def stage_inputs(r, carry):
        r0 = pl.multiple_of(r * ROW_CHUNK, ROW_CHUNK)
        rows = pl.ds(r0, ROW_CHUNK)
        a_v = zbuf[rows, _O_AV:_O_AV + D_A]
        a_g = zbuf[rows, _O_AG:_O_AG + D_A]
        gbuf[pl.ds(r0 + HIST_A, ROW_CHUNK), :] = a_v * jax.nn.sigmoid(a_g)
        ubuf[pl.ds(r0 + HIST_P, ROW_CHUNK), :] = zbuf[rows, _O_UB:_O_UB + D_B]
        vbuf[pl.ds(r0 + HIST_C, ROW_CHUNK), :] = (
            zbuf[rows, _O_CC:_O_CC + D_C] * zbuf[rows, _O_CX:_O_CX + D_C])
        return carry

    lax.fori_loop(0, n_chunks, stage_inputs, 0)

    def mixers(r):
        r0 = r * ROW_CHUNK
        rows = pl.ds(r0, ROW_CHUNK)

        acc = jnp.broadcast_to(cab[...], (ROW_CHUNK, D_A))
        base_a = r0 + (HIST_A - (CONV_A_WIDTH - 1))
        for k in range(CONV_A_WIDTH):
            acc = acc + caw[k:k + 1, :] * gbuf[pl.ds(base_a + k, ROW_CHUNK), :]
        out_a = _layer_norm_silu(acc, lng[...], lnb[...])
        mixbuf[rows, _M_A:_M_A + D_A] = out_a.astype(BF16)

        pos1 = (t * tile + r0 + 1) + lax.broadcasted_iota(jnp.int32, (ROW_CHUNK, LANES), 0)
        u_lo = [ubuf[pl.ds(r0 + HIST_P - j, ROW_CHUNK), 0:LANES] for j in range(4)]
        s2 = u_lo[0] + u_lo[1]
        s4 = s2 + (u_lo[2] + u_lo[3])
        pooled_lo = _pool_select(s2, s4, 2, 4, pos1) - u_lo[0]
        u_hi = [ubuf[pl.ds(r0 + HIST_P - j, ROW_CHUNK), LANES:2 * LANES] for j in range(16)]
        s8 = u_hi[0]
        for j in range(1, 8):
            s8 = s8 + u_hi[j]
        s16 = s8
        for j in range(8, 16):
            s16 = s16 + u_hi[j]
        pooled_hi = _pool_select(s8, s16, 8, 16, pos1) - u_hi[0]
        poolbuf[rows, 0:LANES] = pooled_lo.astype(BF16)
        poolbuf[rows, LANES:2 * LANES] = pooled_hi.astype(BF16)

        conv_c = ccw[2:3, :] * vbuf[pl.ds(r0 + HIST_C, ROW_CHUNK), :]
        conv_c = conv_c + ccw[1:2, :] * vbuf[pl.ds(r0 + HIST_C - 1, ROW_CHUNK), :]
        conv_c = conv_c + ccw[0:1, :] * vbuf[pl.ds(r0 + HIST_C - 2, ROW_CHUNK), :]
        out_c = zbuf[rows, _O_CB:_O_CB + D_C] * conv_c
        mixbuf[rows, _M_C:_M_C + D_C] = out_c.astype(BF16)

    for r in range(n_chunks):
        mixers(r)

    out_b = _dot(poolbuf[...], pool_bd[...]) * pscale[...]
    mixbuf[:, _M_B:_M_B + D_B] = out_b.astype(BF16)

    @pl.when(t == n_t - 1)
    def _():
        na_ref[0] = gbuf[HIST_A + tile - (CONV_A_WIDTH - 1):HIST_A + tile, :]
        np_ref[0] = ubuf[HIST_P + tile - POOL_STATE:HIST_P + tile, :]
        nc_ref[0] = vbuf[HIST_C + tile - (CONV_C_WIDTH - 1):HIST_C + tile, :]

    gbuf[0:HIST_A, :] = gbuf[tile:tile + HIST_A, :]
    ubuf[0:HIST_P, :] = ubuf[tile:tile + HIST_P, :]
    vbuf[0:HIST_C, :] = vbuf[tile:tile + HIST_C, :]

    out_ref[0] = _channel_mix_tail(h, mixbuf[...], p_ref[0], w_out, g_mlp, w_up, w_down,
                                   g_ple, w_pg, w_pp, g_fin, final)


def _sample_kernel(h_ref, p_ref, sa_ref, sp_ref, sc_ref, g_mix, w_in, caw, cab, lng, lnb,
                   pool_bd, pscale, ccw, w_out, g_mlp, w_up, w_down, g_ple, w_pg, w_pp, g_fin,
                   out_ref, na_ref, np_ref, nc_ref, *, pos0, final):
    h = h_ref[...]
    n = _rms_norm(h, g_mix[...]).astype(BF16)
    z = _dot(n, w_in[...])
    a_v = z[:, _O_AV:_O_AV + D_A]
    a_g = z[:, _O_AG:_O_AG + D_A]
    u = z[:, _O_UB:_O_UB + D_B]
    c_b = z[:, _O_CB:_O_CB + D_C]
    v = z[:, _O_CC:_O_CC + D_C] * z[:, _O_CX:_O_CX + D_C]
    glu = a_v * jax.nn.sigmoid(a_g)

    n_hist_a = CONV_A_WIDTH - 1
    acc = cab[...] + caw[n_hist_a:n_hist_a + 1, :] * glu
    for k in range(n_hist_a):
        acc = acc + caw[k:k + 1, :] * sa_ref[:, k * D_A:(k + 1) * D_A]
    out_a = _layer_norm_silu(acc, lng[...], lnb[...])
    na_ref[:, 0:(n_hist_a - 1) * D_A] = sa_ref[:, D_A:n_hist_a * D_A]
    na_ref[:, (n_hist_a - 1) * D_A:n_hist_a * D_A] = glu

    def hist_p(j, lo):
        k = POOL_STATE - j
        return sp_ref[:, k * D_B + lo:k * D_B + lo + LANES]

    rows = h.shape[0]
    pos1 = jnp.full((rows, LANES), pos0 + 1, jnp.int32)
    u_lo = u[:, 0:LANES]
    s2 = u_lo + hist_p(1, 0)
    s4 = s2 + (hist_p(2, 0) + hist_p(3, 0))
    pooled_lo = _pool_select(s2, s4, 2, 4, pos1) - u_lo
    u_hi = u[:, LANES:2 * LANES]
    s8 = u_hi
    for j in range(1, 8):
        s8 = s8 + hist_p(j, LANES)
    s16 = s8
    for j in range(8, 16):
        s16 = s16 + hist_p(j, LANES)
    pooled_hi = _pool_select(s8, s16, 8, 16, pos1) - u_hi
    pooled = jnp.concatenate([pooled_lo, pooled_hi], axis=-1).astype(BF16)
    out_b = _dot(pooled, pool_bd[...]) * pscale[...]
    np_ref[:, 0:(POOL_STATE - 1) * D_B] = sp_ref[:, D_B:POOL_STATE * D_B]
    np_ref[:, (POOL_STATE - 1) * D_B:POOL_STATE * D_B] = u

    conv_c = (ccw[2:3, :] * v + ccw[1:2, :] * sc_ref[:, D_C:2 * D_C]
              + ccw[0:1, :] * sc_ref[:, 0:D_C])
    out_c = c_b * conv_c
    nc_ref[:, 0:D_C] = sc_ref[:, D_C:2 * D_C]
    nc_ref[:, D_C:2 * D_C] = v

    mix = jnp.concatenate([out_a, out_b, out_c], axis=-1).astype(BF16)
    out_ref[...] = _channel_mix_tail(h, mix, p_ref[...], w_out, g_mlp, w_up, w_down,
                                     g_ple, w_pg, w_pp, g_fin, final)


def _resident():
    return pl.BlockSpec(memory_space=pltpu.VMEM)


def _prompt_layer(h, p, weights, *, final):
    batch, seq, _ = h.shape
    tile = min(SEQ_TILE, seq)
    assert seq % tile == 0 and tile % ROW_CHUNK == 0 and tile >= HIST_A
    grid = (batch, seq // tile)
    row_block = lambda width: pl.BlockSpec((1, tile, width), lambda b, t: (b, t, 0))
    state_block = lambda rows, width: pl.BlockSpec((1, rows, width), lambda b, t: (b, 0, 0))
    n_hist_a, n_hist_c = CONV_A_WIDTH - 1, CONV_C_WIDTH - 1
    return pl.pallas_call(
        functools.partial(_prompt_kernel, tile=tile, final=final),
        grid=grid,
        in_specs=[row_block(D_MODEL), row_block(D_PLE)] + [_resident()] * len(weights),
        out_specs=[row_block(D_MODEL), state_block(n_hist_a, D_A),
                   state_block(POOL_STATE, D_B), state_block(n_hist_c, D_C)],
        out_shape=[jax.ShapeDtypeStruct(h.shape, F32),
                   jax.ShapeDtypeStruct((batch, n_hist_a, D_A), F32),
                   jax.ShapeDtypeStruct((batch, POOL_STATE, D_B), F32),
                   jax.ShapeDtypeStruct((batch, n_hist_c, D_C), F32)],
        scratch_shapes=[pltpu.VMEM((tile, D_IN), F32),
                        pltpu.VMEM((HIST_A + tile, D_A), F32),
                        pltpu.VMEM((HIST_P + tile, D_B), F32),
                        pltpu.VMEM((HIST_C + tile, D_C), F32),
                        pltpu.VMEM((tile, D_B), BF16),
                        pltpu.VMEM((tile, D_MODEL), BF16)],
        compiler_params=pltpu.CompilerParams(
            dimension_semantics=("arbitrary", "arbitrary"),
            vmem_limit_bytes=VMEM_LIMIT_BYTES),
        name="prompt_layer",
    )(h, p, *weights)


def _sample_layer(h, p, st_a, st_p, st_c, weights, *, final):
    rows = h.shape[0]
    n_hist_a, n_hist_c = CONV_A_WIDTH - 1, CONV_C_WIDTH - 1
    sa = st_a.reshape(rows, n_hist_a * D_A)
    sp = st_p.reshape(rows, POOL_STATE * D_B)
    sc = st_c.reshape(rows, n_hist_c * D_C)
    out, na, np_, nc = pl.pallas_call(
        functools.partial(_sample_kernel, pos0=PAST_LEN, final=final),
        in_specs=[_resident()] * (5 + len(weights)),
        out_specs=[_resident()] * 4,
        out_shape=[jax.ShapeDtypeStruct(h.shape, F32),
                   jax.ShapeDtypeStruct(sa.shape, F32),
                   jax.ShapeDtypeStruct(sp.shape, F32),
                   jax.ShapeDtypeStruct(sc.shape, F32)],
        compiler_params=pltpu.CompilerParams(vmem_limit_bytes=VMEM_LIMIT_BYTES),
        name="sample_layer",
    )(h, p, sa, sp, sc, *weights)
    return (out, na.reshape(st_a.shape), np_.reshape(st_p.shape), nc.reshape(st_c.shape))


def _pool_block_diag(pool_w):
    groups, c, _ = pool_w.shape
    eye = jnp.eye(groups, dtype=pool_w.dtype)
    return (eye[:, None, :, None] * pool_w[:, :, None, :]).reshape(groups * c, groups * c)


def kernel(x_prompt, x_sample, state_conv_a, state_pool, state_conv_c, p_prompt, p_sample, norm_mix_g, w_in, conv_a_w, conv_a_b, ln_a_g, ln_a_b, pool_w, pool_scale, conv_c_w, w_out, norm_mlp_g, w_up, w_down, norm_ple_g, w_ple_gate, w_ple_proj, final_norm_g):
    depth = w_in.shape[0]
    dec_batch, dec_seq, _ = x_sample.shape
    assert dec_seq == 1
    hp = x_prompt
    hs = x_sample.reshape(dec_batch, D_MODEL)
    row = lambda v: v.reshape(1, -1)
    prompt_states, sample_states = [], []
    for i in range(depth):
        final = i == depth - 1
        weights = (row(norm_mix_g[i]), w_in[i].astype(BF16), conv_a_w[i], row(conv_a_b[i]),
                   row(ln_a_g[i]), row(ln_a_b[i]), _pool_block_diag(pool_w[i]).astype(BF16),
                   row(pool_scale[i]), conv_c_w[i], w_out[i].astype(BF16), row(norm_mlp_g[i]),
                   w_up[i].astype(BF16), w_down[i].astype(BF16), row(norm_ple_g[i]),
                   w_ple_gate[i].astype(BF16), w_ple_proj[i].astype(BF16), row(final_norm_g))
        hp, *st = _prompt_layer(hp, p_prompt[i], weights, final=final)
        prompt_states.append(st)
        hs, *st = _sample_layer(hs, p_sample[i].reshape(dec_batch, D_PLE), state_conv_a[i],
                                state_pool[i], state_conv_c[i], weights, final=final)
        sample_states.append(st)
    stack = lambda states, j: jnp.stack([s[j] for s in states])
    return (hp, hs.reshape(x_sample.shape),
            stack(prompt_states, 0), stack(prompt_states, 1), stack(prompt_states, 2),
            stack(sample_states, 0), stack(sample_states, 1), stack(sample_states, 2))
```

```python
import functools

import jax
import jax.numpy as jnp
from jax import lax
from jax.experimental import pallas as pl
from jax.experimental.pallas import tpu as pltpu

D_MODEL = 1024
D_A = 384
D_B = 256
D_C = 384
D_IN = 2 * D_A + D_B + 3 * D_C
D_FF = 4096
D_PLE = 256
POOL_WINDOWS = (2, 4, 8, 16)
POOL_GROUP_DIM = D_B // len(POOL_WINDOWS)
POOL_STATE = 15
CONV_A_WIDTH = 31
CONV_C_WIDTH = 3
N_HIST_A = CONV_A_WIDTH - 1
N_HIST_C = CONV_C_WIDTH - 1
PAST_LEN = 16384
EPS = 1e-6

_O_AV = 0
_O_AG = D_A
_O_UB = 2 * D_A
_O_CB = _O_UB + D_B
_O_CC = _O_CB + D_C
_O_CX = _O_CC + D_C
_M_A = 0
_M_B = D_A
_M_C = D_A + D_B

LANES = 128
HIST_A = 32
HIST_P = 16
HIST_C = 8
ROW_CHUNK = 64
FF_SLICE = 512
FF_CHUNK = 1024
SEQ_TILE = 512
VMEM_LIMIT_BYTES = 60 * 1024 * 1024

BF16 = jnp.bfloat16
F32 = jnp.float32


def _rms_norm(x, g):
    ms = jnp.mean(x * x, axis=-1, keepdims=True)
    return x * lax.rsqrt(ms + EPS) * g


def _dot(a, b):
    return jnp.dot(a, b, preferred_element_type=F32)


def _layer_norm_silu(ya, g, b):
    mu = jnp.mean(ya, axis=-1, keepdims=True)
    xc = ya - mu
    var = jnp.mean(xc * xc, axis=-1, keepdims=True)
    y = xc * lax.rsqrt(var + EPS) * g + b
    return y * jax.nn.sigmoid(y)


def _pool_select(s_small, s_big, w_small, w_big, pos1):
    lane = lax.broadcasted_iota(jnp.int32, s_small.shape, 1)
    cnt_small = jnp.minimum(pos1, w_small).astype(F32)
    cnt_big = jnp.minimum(pos1, w_big).astype(F32)
    return jnp.where(lane < POOL_GROUP_DIM, s_small / cnt_small, s_big / cnt_big)


def _lane_cols(offset, c):
    return slice(offset + c * LANES, offset + (c + 1) * LANES)


def _channel_mix_tail(h, mix_bf16, p, w_out, g_mlp, w_up, w_down, g_ple, w_pg, w_pp, g_fin, final):
    h = h + _dot(mix_bf16, w_out[...])
    m = _rms_norm(h, g_mlp[...]).astype(BF16)
    acc = None
    for c in range(D_FF // FF_CHUNK):
        cols = slice(c * FF_CHUNK, (c + 1) * FF_CHUNK)
        up = _dot(m, w_up[:, cols])
        act = jnp.square(jnp.maximum(up, 0.0)).astype(BF16)
        down = _dot(act, w_down[cols, :])
        acc = down if acc is None else acc + down
    h = h + acc
    r = _rms_norm(h, g_ple[...]).astype(BF16)
    gate = jax.nn.sigmoid(_dot(r, w_pg[...]))
    h = h + gate * _dot(p.astype(BF16), w_pp[...])
    if final:
        h = _rms_norm(h, g_fin[...])
    return h


def _prompt_kernel(h_ref, p_ref, g_mix, w_in, caw, cab, lng, lnb, pool_bd, pscale, ccw,
                   w_out, g_mlp, w_up, w_down, g_ple, w_pg, w_pp, g_fin,
                   out_ref, na_ref, np_ref, nc_ref,
                   zbuf, gbuf, ubuf, vbuf, poolbuf, mixbuf, hbuf,
                   *, tile, n_tiles, tiles_per_seq, final):
    s = pl.program_id(0)
    cur = s % 2
    prev = 1 - cur
    t = s % tiles_per_seq

    @pl.when(s == 0)
    def _():
        hbuf[1] = jnp.zeros((tile, D_MODEL), F32)
        mixbuf[1] = jnp.zeros((tile, D_MODEL), BF16)

    @pl.when(t == 0)
    def _():
        gbuf[:, 0:HIST_A, :] = jnp.zeros((D_A // LANES, HIST_A, LANES), F32)
        ubuf[:, 0:HIST_P, :] = jnp.zeros((D_B // LANES, HIST_P, LANES), F32)
        vbuf[:, 0:HIST_C, :] = jnp.zeros((D_C // LANES, HIST_C, LANES), F32)

    h_new = h_ref[...]
    hbuf[cur] = h_new
    zbuf[...] = _dot(_rms_norm(h_new, g_mix[...]).astype(BF16), w_in[...])

    h1 = hbuf[prev] + _dot(mixbuf[prev], w_out[...])
    m = _rms_norm(h1, g_mlp[...]).astype(BF16)

    def shifted(buf, c, start):
        return buf[c, pl.ds(start, ROW_CHUNK, stride=1), :]

    def mixers(r0):
        rows = pl.ds(r0, ROW_CHUNK)
        for c in range(D_A // LANES):
            glu = zbuf[rows, _lane_cols(_O_AV, c)] * jax.nn.sigmoid(zbuf[rows, _lane_cols(_O_AG, c)])
            gbuf[c, pl.ds(r0 + HIST_A, ROW_CHUNK), :] = glu
        for c in range(D_B // LANES):
            ubuf[c, pl.ds(r0 + HIST_P, ROW_CHUNK), :] = zbuf[rows, _lane_cols(_O_UB, c)]
        for c in range(D_C // LANES):
            vbuf[c, pl.ds(r0 + HIST_C, ROW_CHUNK), :] = (
                zbuf[rows, _lane_cols(_O_CC, c)] * zbuf[rows, _lane_cols(_O_CX, c)])

        conv_a = []
        for c in range(D_A // LANES):
            acc = jnp.broadcast_to(cab[:, _lane_cols(0, c)], (ROW_CHUNK, LANES))
            for k in range(CONV_A_WIDTH):
                acc = acc + caw[k:k + 1, _lane_cols(0, c)] * shifted(gbuf, c, r0 + HIST_A - N_HIST_A + k)
            conv_a.append(acc)
        out_a = _layer_norm_silu(jnp.concatenate(conv_a, axis=-1), lng[...], lnb[...])
        mixbuf[cur, rows, _M_A:_M_A + D_A] = out_a.astype(BF16)

        pos1 = (t * tile + r0 + 1) + lax.broadcasted_iota(jnp.int32, (ROW_CHUNK, LANES), 0)
        u_lo = [shifted(ubuf, 0, r0 + HIST_P - j) for j in range(4)]
        s2 = u_lo[0] + u_lo[1]
        s4 = s2 + (u_lo[2] + u_lo[3])
        pooled_lo = _pool_select(s2, s4, 2, 4, pos1) - u_lo[0]
        u_hi = [shifted(ubuf, 1, r0 + HIST_P - j) for j in range(16)]
        s8 = u_hi[0]
        for j in range(1, 8):
            s8 = s8 + u_hi[j]
        s16 = s8
        for j in range(8, 16):
            s16 = s16 + u_hi[j]
        pooled_hi = _pool_select(s8, s16, 8, 16, pos1) - u_hi[0]
        poolbuf[rows, 0:LANES] = pooled_lo.astype(BF16)
        poolbuf[rows, LANES:2 * LANES] = pooled_hi.astype(BF16)

        for c in range(D_C // LANES):
            conv_c = ccw[2:3, _lane_cols(0, c)] * shifted(vbuf, c, r0 + HIST_C)
            conv_c = conv_c + ccw[1:2, _lane_cols(0, c)] * shifted(vbuf, c, r0 + HIST_C - 1)
            conv_c = conv_c + ccw[0:1, _lane_cols(0, c)] * shifted(vbuf, c, r0 + HIST_C - 2)
            out_c = zbuf[rows, _lane_cols(_O_CB, c)] * conv_c
            mixbuf[cur, rows, _lane_cols(_M_C, c)] = out_c.astype(BF16)

    def up_act(j):
        up = _dot(m, w_up[:, j * FF_SLICE:(j + 1) * FF_SLICE])
        return jnp.square(jnp.maximum(up, 0.0)).astype(BF16)

    n_ff = D_FF // FF_SLICE
    chunks_per_slice = tile // ROW_CHUNK // n_ff
    h2 = h1
    act = up_act(0)
    for j in range(n_ff):
        for q in range(chunks_per_slice):
            mixers((j * chunks_per_slice + q) * ROW_CHUNK)
        down = _dot(act, w_down[j * FF_SLICE:(j + 1) * FF_SLICE, :])
        if j + 1 < n_ff:
            act = up_act(j + 1)
        h2 = h2 + down

    out_b = _dot(poolbuf[...], pool_bd[...]) * pscale[...]
    mixbuf[cur, :, _M_B:_M_B + D_B] = out_b.astype(BF16)

    @pl.when((t == tiles_per_seq - 1) & (s < n_tiles))
    def _():
        for c in range(D_A // LANES):
            na_ref[:, _lane_cols(0, c)] = gbuf[c, HIST_A + tile - N_HIST_A:HIST_A + tile, :]
        for c in range(D_B // LANES):
            np_ref[:, _lane_cols(0, c)] = ubuf[c, HIST_P + tile - POOL_STATE:HIST_P + tile, :]
        for c in range(D_C // LANES):
            nc_ref[:, _lane_cols(0, c)] = vbuf[c, HIST_C + tile - N_HIST_C:HIST_C + tile, :]

    gbuf[:, 0:HIST_A, :] = gbuf[:, tile:tile + HIST_A, :]
    ubuf[:, 0:HIST_P, :] = ubuf[:, tile:tile + HIST_P, :]
    vbuf[:, 0:HIST_C, :] = vbuf[:, tile:tile + HIST_C, :]

    r3 = _rms_norm(h2, g_ple[...]).astype(BF16)
    gate = jax.nn.sigmoid(_dot(r3, w_pg[...]))
    h3 = h2 + gate * _dot(p_ref[...].astype(BF16), w_pp[...])
    if final:
        h3 = _rms_norm(h3, g_fin[...])
    out_ref[...] = h3


def _sample_kernel(h_ref, p_ref, sa_ref, sp_ref, sc_ref, g_mix, w_in, caw, cab, lng, lnb,
                   pool_bd, pscale, ccw, w_out, g_mlp, w_up, w_down, g_ple, w_pg, w_pp, g_fin,
                   out_ref, na_ref, np_ref, nc_ref, *, pos0, final):
    h = h_ref[...]
    n = _rms_norm(h, g_mix[...]).astype(BF16)
    z = _dot(n, w_in[...])
    a_v = z[:, _O_AV:_O_AV + D_A]
    a_g = z[:, _O_AG:_O_AG + D_A]
    u = z[:, _O_UB:_O_UB + D_B]
    c_b = z[:, _O_CB:_O_CB + D_C]
    v = z[:, _O_CC:_O_CC + D_C] * z[:, _O_CX:_O_CX + D_C]
    glu = a_v * jax.nn.sigmoid(a_g)

    acc = cab[...] + caw[N_HIST_A:N_HIST_A + 1, :] * glu
    for k in range(N_HIST_A):
        acc = acc + caw[k:k + 1, :] * sa_ref[:, k * D_A:(k + 1) * D_A]
    out_a = _layer_norm_silu(acc, lng[...], lnb[...])
    na_ref[:, 0:(N_HIST_A - 1) * D_A] = sa_ref[:, D_A:N_HIST_A * D_A]
    na_ref[:, (N_HIST_A - 1) * D_A:N_HIST_A * D_A] = glu

    def hist_p(j, lo):
        k = POOL_STATE - j
        return sp_ref[:, k * D_B + lo:k * D_B + lo + LANES]

    rows = h.shape[0]
    pos1 = jnp.full((rows, LANES), pos0 + 1, jnp.int32)
    u_lo = u[:, 0:LANES]
    s2 = u_lo + hist_p(1, 0)
    s4 = s2 + (hist_p(2, 0) + hist_p(3, 0))
    pooled_lo = _pool_select(s2, s4, 2, 4, pos1) - u_lo
    u_hi = u[:, LANES:2 * LANES]
    s8 = u_hi
    for j in range(1, 8):
        s8 = s8 + hist_p(j, LANES)
    s16 = s8
    for j in range(8, 16):
        s16 = s16 + hist_p(j, LANES)
    pooled_hi = _pool_select(s8, s16, 8, 16, pos1) - u_hi
    pooled = jnp.concatenate([pooled_lo, pooled_hi], axis=-1).astype(BF16)
    out_b = _dot(pooled, pool_bd[...]) * pscale[...]
    np_ref[:, 0:(POOL_STATE - 1) * D_B] = sp_ref[:, D_B:POOL_STATE * D_B]
    np_ref[:, (POOL_STATE - 1) * D_B:POOL_STATE * D_B] = u

    conv_c = (ccw[2:3, :] * v + ccw[1:2, :] * sc_ref[:, D_C:2 * D_C]
              + ccw[0:1, :] * sc_ref[:, 0:D_C])
    out_c = c_b * conv_c
    nc_ref[:, 0:D_C] = sc_ref[:, D_C:2 * D_C]
    nc_ref[:, D_C:2 * D_C] = v

    mix = jnp.concatenate([out_a, out_b, out_c], axis=-1).astype(BF16)
    out_ref[...] = _channel_mix_tail(h, mix, p_ref[...], w_out, g_mlp, w_up, w_down,
                                     g_ple, w_pg, w_pp, g_fin, final)


def _resident():
    return pl.BlockSpec(memory_space=pltpu.VMEM)


def _prompt_layer(h, p_all, layer, weights, *, final):
    batch, seq, _ = h.shape
    tile = min(SEQ_TILE, seq)
    assert seq % tile == 0 and tile % (ROW_CHUNK * (D_FF // FF_SLICE)) == 0 and tile >= HIST_A
    tiles_per_seq = seq // tile
    n_tiles = batch * tiles_per_seq

    def front(s):
        i = jnp.minimum(s, n_tiles - 1)
        return i // tiles_per_seq, i % tiles_per_seq

    def back(s):
        i = jnp.maximum(s - 1, 0)
        return i // tiles_per_seq, i % tiles_per_seq

    h_in = pl.BlockSpec((None, tile, D_MODEL), lambda s: (*front(s), 0))
    p_in = pl.BlockSpec((None, None, tile, D_PLE), lambda s: (layer, *back(s), 0))
    h_out = pl.BlockSpec((None, tile, D_MODEL), lambda s: (*back(s), 0))
    state_block = lambda rows, width: pl.BlockSpec((None, rows, width), lambda s: (front(s)[0], 0, 0))
    return pl.pallas_call(
        functools.partial(_prompt_kernel, tile=tile, n_tiles=n_tiles, tiles_per_seq=tiles_per_seq,
                          final=final),
        grid=(n_tiles + 1,),
        in_specs=[h_in, p_in] + [_resident()] * len(weights),
        out_specs=[h_out, state_block(N_HIST_A, D_A),
                   state_block(POOL_STATE, D_B), state_block(N_HIST_C, D_C)],
        out_shape=[jax.ShapeDtypeStruct(h.shape, F32),
                   jax.ShapeDtypeStruct((batch, N_HIST_A, D_A), F32),
                   jax.ShapeDtypeStruct((batch, POOL_STATE, D_B), F32),
                   jax.ShapeDtypeStruct((batch, N_HIST_C, D_C), F32)],
        scratch_shapes=[pltpu.VMEM((tile, D_IN), F32),
                        pltpu.VMEM((D_A // LANES, HIST_A + tile, LANES), F32),
                        pltpu.VMEM((D_B // LANES, HIST_P + tile, LANES), F32),
                        pltpu.VMEM((D_C // LANES, HIST_C + tile, LANES), F32),
                        pltpu.VMEM((tile, D_B), BF16),
                        pltpu.VMEM((2, tile, D_MODEL), BF16),
                        pltpu.VMEM((2, tile, D_MODEL), F32)],
        compiler_params=pltpu.CompilerParams(
            dimension_semantics=("arbitrary",),
            vmem_limit_bytes=VMEM_LIMIT_BYTES),
        name="prompt_layer",
    )(h, p_all, *weights)


def _sample_layer(h, p, st_a, st_p, st_c, weights, *, final):
    rows = h.shape[0]
    sa = st_a.reshape(rows, N_HIST_A * D_A)
    sp = st_p.reshape(rows, POOL_STATE * D_B)
    sc = st_c.reshape(rows, N_HIST_C * D_C)
    out, na, np_, nc = pl.pallas_call(
        functools.partial(_sample_kernel, pos0=PAST_LEN, final=final),
        in_specs=[_resident()] * (5 + len(weights)),
        out_specs=[_resident()] * 4,
        out_shape=[jax.ShapeDtypeStruct(h.shape, F32),
                   jax.ShapeDtypeStruct(sa.shape, F32),
                   jax.ShapeDtypeStruct(sp.shape, F32),
                   jax.ShapeDtypeStruct(sc.shape, F32)],
        compiler_params=pltpu.CompilerParams(vmem_limit_bytes=VMEM_LIMIT_BYTES),
        name="sample_layer",
    )(h, p, sa, sp, sc, *weights)
    return (out, na.reshape(st_a.shape), np_.reshape(st_p.shape), nc.reshape(st_c.shape))


def _pool_block_diag(pool_w):
    groups, c, _ = pool_w.shape
    eye = jnp.eye(groups, dtype=pool_w.dtype)
    return (eye[:, None, :, None] * pool_w[:, :, None, :]).reshape(groups * c, groups * c)


def kernel(x_prompt, x_sample, state_conv_a, state_pool, state_conv_c, p_prompt, p_sample, norm_mix_g, w_in, conv_a_w, conv_a_b, ln_a_g, ln_a_b, pool_w, pool_scale, conv_c_w, w_out, norm_mlp_g, w_up, w_down, norm_ple_g, w_ple_gate, w_ple_proj, final_norm_g):
    depth = w_in.shape[0]
    dec_batch, dec_seq, _ = x_sample.shape
    assert dec_seq == 1
    hp = x_prompt
    hs = x_sample.reshape(dec_batch, D_MODEL)
    row = lambda v: v.reshape(1, -1)
    prompt_states, sample_states = [], []
    for i in range(depth):
        final = i == depth - 1
        weights = (row(norm_mix_g[i]), w_in[i].astype(BF16), conv_a_w[i], row(conv_a_b[i]),
                   row(ln_a_g[i]), row(ln_a_b[i]), _pool_block_diag(pool_w[i]).astype(BF16),
                   row(pool_scale[i]), conv_c_w[i], w_out[i].astype(BF16), row(norm_mlp_g[i]),
                   w_up[i].astype(BF16), w_down[i].astype(BF16), row(norm_ple_g[i]),
                   w_ple_gate[i].astype(BF16), w_ple_proj[i].astype(BF16), row(final_norm_g))
        hp, *st = _prompt_layer(hp, p_prompt, i, weights, final=final)
        prompt_states.append(st)
        hs, *st = _sample_layer(hs, p_sample[i].reshape(dec_batch, D_PLE), state_conv_a[i],
                                state_pool[i], state_conv_c[i], weights, final=final)
        sample_states.append(st)
    stack = lambda states, j: jnp.stack([s[j] for s in states])
    return (hp, hs.reshape(x_sample.shape),
            stack(prompt_states, 0), stack(prompt_states, 1), stack(prompt_states, 2),
            stack(sample_states, 0), stack(sample_states, 1), stack(sample_states, 2))
```

```python
import functools

import jax
import jax.numpy as jnp
from jax import lax
from jax.experimental import pallas as pl
from jax.experimental.pallas import tpu as pltpu

D_MODEL = 1024
D_A = 384
D_B = 256
D_C = 384
D_IN = 2 * D_A + D_B + 3 * D_C
D_FF = 4096
D_PLE = 256
POOL_WINDOWS = (2, 4, 8, 16)
POOL_GROUP_DIM = D_B // len(POOL_WINDOWS)
POOL_STATE = 15
CONV_A_WIDTH = 31
CONV_C_WIDTH = 3
N_HIST_A = CONV_A_WIDTH - 1
N_HIST_C = CONV_C_WIDTH - 1
PAST_LEN = 16384
EPS = 1e-6

_O_AV = 0
_O_AG = D_A
_O_UB = 2 * D_A
_O_CB = _O_UB + D_B
_O_CC = _O_CB + D_C
_O_CX = _O_CC + D_C
_M_A = 0
_M_B = D_A
_M_C = D_A + D_B

LANES = 128
HIST_A = 32
HIST_P = 16
HIST_C = 8
ROW_CHUNK = 64
SAMPLE_ROW_CHUNK = 32
FF_SLICE = 1024
SEQ_TILE = 512
VMEM_LIMIT_BYTES = 60 * 1024 * 1024

BF16 = jnp.bfloat16
F32 = jnp.float32


def _rms_norm(x, g):
    ms = jnp.mean(x * x, axis=-1, keepdims=True)
    return x * lax.rsqrt(ms + EPS) * g


def _dot(a, b):
    return jnp.dot(a, b, preferred_element_type=F32)


def _layer_norm_silu(ya, g, b):
    mu = jnp.mean(ya, axis=-1, keepdims=True)
    xc = ya - mu
    var = jnp.mean(xc * xc, axis=-1, keepdims=True)
    y = xc * lax.rsqrt(var + EPS) * g + b
    return y * jax.nn.sigmoid(y)


def _pool_select(s_small, s_big, w_small, w_big, pos1):
    lane = lax.broadcasted_iota(jnp.int32, s_small.shape, 1)
    cnt_small = jnp.minimum(pos1, w_small).astype(F32)
    cnt_big = jnp.minimum(pos1, w_big).astype(F32)
    return jnp.where(lane < POOL_GROUP_DIM, s_small / cnt_small, s_big / cnt_big)


def _lane_cols(offset, c):
    return slice(offset + c * LANES, offset + (c + 1) * LANES)


def _mlp(h1, g_mlp, w_up, w_down, interleave=None):
    m = _rms_norm(h1, g_mlp[...]).astype(BF16)

    def up_act(j):
        up = _dot(m, w_up[:, j * FF_SLICE:(j + 1) * FF_SLICE])
        return jnp.square(jnp.maximum(up, 0.0)).astype(BF16)

    n_ff = D_FF // FF_SLICE
    h2 = h1
    act = up_act(0)
    for j in range(n_ff):
        if interleave is not None:
            interleave(j)
        down = _dot(act, w_down[j * FF_SLICE:(j + 1) * FF_SLICE, :])
        if j + 1 < n_ff:
            act = up_act(j + 1)
        h2 = h2 + down
    return h2


def _gated_embedding(h2, p, g_ple, w_pg, w_pp, g_fin, final):
    r3 = _rms_norm(h2, g_ple[...]).astype(BF16)
    gate = jax.nn.sigmoid(_dot(r3, w_pg[...]))
    h3 = h2 + gate * _dot(p.astype(BF16), w_pp[...])
    if final:
        h3 = _rms_norm(h3, g_fin[...])
    return h3


def _prompt_kernel(h_ref, p_ref, g_mix, w_in, caw, cab, lng, lnb, pool_bd, pscale, ccw,
                   w_out, g_mlp, w_up, w_down, g_ple, w_pg, w_pp, g_fin, *rest,
                   tile, n_tiles, tiles_per_seq, n_aliased, final):
    out_ref, na_ref, np_ref, nc_ref, zbuf, gbuf, ubuf, vbuf, poolbuf, mixbuf, hbuf = rest[n_aliased:]
    s = pl.program_id(0)
    cur = s % 2
    prev = 1 - cur
    t = s % tiles_per_seq

    def shifted(buf, c, start):
        return buf[c, pl.ds(start, ROW_CHUNK, stride=1), :]

    def mixers(r0):
        rows = pl.ds(r0, ROW_CHUNK)
        for c in range(D_A // LANES):
            glu = zbuf[rows, _lane_cols(_O_AV, c)] * jax.nn.sigmoid(zbuf[rows, _lane_cols(_O_AG, c)])
            gbuf[c, pl.ds(r0 + HIST_A, ROW_CHUNK), :] = glu
        for c in range(D_B // LANES):
            ubuf[c, pl.ds(r0 + HIST_P, ROW_CHUNK), :] = zbuf[rows, _lane_cols(_O_UB, c)]
        for c in range(D_C // LANES):
            vbuf[c, pl.ds(r0 + HIST_C, ROW_CHUNK), :] = (
                zbuf[rows, _lane_cols(_O_CC, c)] * zbuf[rows, _lane_cols(_O_CX, c)])

        conv_a = []
        for c in range(D_A // LANES):
            acc = jnp.broadcast_to(cab[:, _lane_cols(0, c)], (ROW_CHUNK, LANES))
            for k in range(CONV_A_WIDTH):
                acc = acc + caw[k:k + 1, _lane_cols(0, c)] * shifted(gbuf, c, r0 + HIST_A - N_HIST_A + k)
            conv_a.append(acc)
        out_a = _layer_norm_silu(jnp.concatenate(conv_a, axis=-1), lng[...], lnb[...])
        mixbuf[cur, rows, _M_A:_M_A + D_A] = out_a.astype(BF16)

        pos1 = (t * tile + r0 + 1) + lax.broadcasted_iota(jnp.int32, (ROW_CHUNK, LANES), 0)
        u_lo = [shifted(ubuf, 0, r0 + HIST_P - j) for j in range(4)]
        s2 = u_lo[0] + u_lo[1]
        s4 = s2 + (u_lo[2] + u_lo[3])
        pooled_lo = _pool_select(s2, s4, 2, 4, pos1) - u_lo[0]
        u_hi = [shifted(ubuf, 1, r0 + HIST_P - j) for j in range(16)]
        s8 = u_hi[0]
        for j in range(1, 8):
            s8 = s8 + u_hi[j]
        s16 = s8
        for j in range(8, 16):
            s16 = s16 + u_hi[j]
        pooled_hi = _pool_select(s8, s16, 8, 16, pos1) - u_hi[0]
        poolbuf[rows, 0:LANES] = pooled_lo.astype(BF16)
        poolbuf[rows, LANES:2 * LANES] = pooled_hi.astype(BF16)

        for c in range(D_C // LANES):
            conv_c = ccw[2:3, _lane_cols(0, c)] * shifted(vbuf, c, r0 + HIST_C)
            conv_c = conv_c + ccw[1:2, _lane_cols(0, c)] * shifted(vbuf, c, r0 + HIST_C - 1)
            conv_c = conv_c + ccw[0:1, _lane_cols(0, c)] * shifted(vbuf, c, r0 + HIST_C - 2)
            out_c = zbuf[rows, _lane_cols(_O_CB, c)] * conv_c
            mixbuf[cur, rows, _lane_cols(_M_C, c)] = out_c.astype(BF16)

    n_ff = D_FF // FF_SLICE
    chunks_per_slice = tile // ROW_CHUNK // n_ff

    def mixer_blocks(j):
        for q in range(chunks_per_slice):
            mixers((j * chunks_per_slice + q) * ROW_CHUNK)

    def step(front, back):
        if front:
            @pl.when(t == 0)
            def _():
                gbuf[:, 0:HIST_A, :] = jnp.zeros((D_A // LANES, HIST_A, LANES), F32)
                ubuf[:, 0:HIST_P, :] = jnp.zeros((D_B // LANES, HIST_P, LANES), F32)
                vbuf[:, 0:HIST_C, :] = jnp.zeros((D_C // LANES, HIST_C, LANES), F32)

            h_new = h_ref[...]
            hbuf[cur] = h_new
            zbuf[...] = _dot(_rms_norm(h_new, g_mix[...]).astype(BF16), w_in[...])

        if back:
            h1 = hbuf[prev] + _dot(mixbuf[prev], w_out[...])
            h2 = _mlp(h1, g_mlp, w_up, w_down, interleave=mixer_blocks if front else None)
        else:
            for j in range(n_ff):
                mixer_blocks(j)

        if front:
            out_b = _dot(poolbuf[...], pool_bd[...]) * pscale[...]
            mixbuf[cur, :, _M_B:_M_B + D_B] = out_b.astype(BF16)

            @pl.when(t == tiles_per_seq - 1)
            def _():
                for c in range(D_A // LANES):
                    na_ref[:, _lane_cols(0, c)] = gbuf[c, HIST_A + tile - N_HIST_A:HIST_A + tile, :]
                for c in range(D_B // LANES):
                    np_ref[:, _lane_cols(0, c)] = ubuf[c, HIST_P + tile - POOL_STATE:HIST_P + tile, :]
                for c in range(D_C // LANES):
                    nc_ref[:, _lane_cols(0, c)] = vbuf[c, HIST_C + tile - N_HIST_C:HIST_C + tile, :]

            gbuf[:, 0:HIST_A, :] = gbuf[:, tile:tile + HIST_A, :]
            ubuf[:, 0:HIST_P, :] = ubuf[:, tile:tile + HIST_P, :]
            vbuf[:, 0:HIST_C, :] = vbuf[:, tile:tile + HIST_C, :]

        if back:
            out_ref[...] = _gated_embedding(h2, p_ref[...], g_ple, w_pg, w_pp, g_fin, final)

    pl.when(s == 0)(lambda: step(True, False))
    pl.when((s > 0) & (s < n_tiles))(lambda: step(True, True))
    pl.when(s == n_tiles)(lambda: step(False, True))


def _sample_kernel(h_ref, p_ref, sa_ref, sp_ref, sc_ref, g_mix, w_in, caw, cab, lng, lnb,
                   pool_bd, pscale, ccw, w_out, g_mlp, w_up, w_down, g_ple, w_pg, w_pp, g_fin, *rest,
                   pos0, n_aliased, final):
    out_ref, na_ref, np_ref, nc_ref, mixbuf = rest[n_aliased:]
    h = h_ref[...]
    rows = h.shape[0]
    z = _dot(_rms_norm(h, g_mix[...]).astype(BF16), w_in[...])
    u = z[:, _O_UB:_O_UB + D_B]
    v = z[:, _O_CC:_O_CC + D_C] * z[:, _O_CX:_O_CX + D_C]
    glu = z[:, _O_AV:_O_AV + D_A] * jax.nn.sigmoid(z[:, _O_AG:_O_AG + D_A])

    for r0 in range(0, rows, SAMPLE_ROW_CHUNK):
        blk = slice(r0, r0 + SAMPLE_ROW_CHUNK)
        acc = cab[...] + caw[N_HIST_A:N_HIST_A + 1, :] * glu[blk]
        for k in range(N_HIST_A):
            acc = acc + caw[k:k + 1, :] * sa_ref[k, blk, :]
        mixbuf[blk, _M_A:_M_A + D_A] = _layer_norm_silu(acc, lng[...], lnb[...]).astype(BF16)
    na_ref[0:N_HIST_A - 1] = sa_ref[1:N_HIST_A]
    na_ref[N_HIST_A - 1] = glu

    pos1 = jnp.full((rows, LANES), pos0 + 1, jnp.int32)
    u_lo = u[:, 0:LANES]
    s2 = u_lo + sp_ref[POOL_STATE - 1, :, 0:LANES]
    s4 = s2 + (sp_ref[POOL_STATE - 2, :, 0:LANES] + sp_ref[POOL_STATE - 3, :, 0:LANES])
    pooled_lo = _pool_select(s2, s4, 2, 4, pos1) - u_lo
    u_hi = u[:, LANES:2 * LANES]
    s8 = u_hi
    for j in range(1, 8):
        s8 = s8 + sp_ref[POOL_STATE - j, :, LANES:2 * LANES]
    s16 = s8
    for j in range(8, 16):
        s16 = s16 + sp_ref[POOL_STATE - j, :, LANES:2 * LANES]
    pooled_hi = _pool_select(s8, s16, 8, 16, pos1) - u_hi
    pooled = jnp.concatenate([pooled_lo, pooled_hi], axis=-1).astype(BF16)
    mixbuf[:, _M_B:_M_B + D_B] = (_dot(pooled, pool_bd[...]) * pscale[...]).astype(BF16)
    np_ref[0:POOL_STATE - 1] = sp_ref[1:POOL_STATE]
    np_ref[POOL_STATE - 1] = u

    conv_c = ccw[2:3, :] * v + ccw[1:2, :] * sc_ref[1] + ccw[0:1, :] * sc_ref[0]
    mixbuf[:, _M_C:_M_C + D_C] = (z[:, _O_CB:_O_CB + D_C] * conv_c).astype(BF16)
    nc_ref[0] = sc_ref[1]
    nc_ref[1] = v

    h1 = h + _dot(mixbuf[...], w_out[...])
    h2 = _mlp(h1, g_mlp, w_up, w_down)
    out_ref[...] = _gated_embedding(h2, p_ref[...], g_ple, w_pg, w_pp, g_fin, final)


def _resident():
    return pl.BlockSpec(memory_space=pltpu.VMEM)


def _layer_resident(arr, layer):
    zeros = (0,) * (arr.ndim - 1)
    return pl.BlockSpec((None,) + arr.shape[1:], lambda s: (layer,) + zeros, pipeline_mode=pl.Buffered(1))


def _untouched():
    return pl.BlockSpec(memory_space=pl.ANY)


def _prompt_layer(h, p_all, layer, weights, prev_states, *, final):
    depth = p_all.shape[0]
    batch, seq, _ = h.shape
    tile = min(SEQ_TILE, seq)
    assert seq % tile == 0 and tile % (ROW_CHUNK * (D_FF // FF_SLICE)) == 0 and tile >= HIST_A
    tiles_per_seq = seq // tile
    n_tiles = batch * tiles_per_seq

    def front(s):
        i = jnp.minimum(s, n_tiles - 1)
        return i // tiles_per_seq, i % tiles_per_seq

    def back(s):
        i = jnp.maximum(s - 1, 0)
        return i // tiles_per_seq, i % tiles_per_seq

    h_in = pl.BlockSpec((None, tile, D_MODEL), lambda s: (*front(s), 0))
    p_in = pl.BlockSpec((None, None, tile, D_PLE), lambda s: (layer, *back(s), 0))
    h_out = pl.BlockSpec((None, tile, D_MODEL), lambda s: (*back(s), 0))
    state_block = lambda rows, width: pl.BlockSpec(
        (None, None, rows, width), lambda s: (layer, front(s)[0], 0, 0))
    aliased = () if prev_states is None else tuple(prev_states)
    n_in = 2 + len(weights)
    return pl.pallas_call(
        functools.partial(_prompt_kernel, tile=tile, n_tiles=n_tiles, tiles_per_seq=tiles_per_seq,
                          n_aliased=len(aliased), final=final),
        grid=(n_tiles + 1,),
        in_specs=[h_in, p_in] + [_layer_resident(w, layer) for w in weights] + [_untouched()] * len(aliased),
        out_specs=[h_out, state_block(N_HIST_A, D_A),
                   state_block(POOL_STATE, D_B), state_block(N_HIST_C, D_C)],
        out_shape=[jax.ShapeDtypeStruct(h.shape, F32),
                   jax.ShapeDtypeStruct((depth, batch, N_HIST_A, D_A), F32),
                   jax.ShapeDtypeStruct((depth, batch, POOL_STATE, D_B), F32),
                   jax.ShapeDtypeStruct((depth, batch, N_HIST_C, D_C), F32)],
        input_output_aliases={n_in + k: 1 + k for k in range(len(aliased))},
        scratch_shapes=[pltpu.VMEM((tile, D_IN), F32),
                        pltpu.VMEM((D_A // LANES, HIST_A + tile, LANES), F32),
                        pltpu.VMEM((D_B // LANES, HIST_P + tile, LANES), F32),
                        pltpu.VMEM((D_C // LANES, HIST_C + tile, LANES), F32),
                        pltpu.VMEM((tile, D_B), BF16),
                        pltpu.VMEM((2, tile, D_MODEL), BF16),
                        pltpu.VMEM((2, tile, D_MODEL), F32)],
        compiler_params=pltpu.CompilerParams(
            dimension_semantics=("arbitrary",),
            vmem_limit_bytes=VMEM_LIMIT_BYTES),
        name="prompt_layer",
    )(h, p_all, *weights, *aliased)


def _sample_layer(h, p_all, st_a, st_p, st_c, layer, weights, prev_states, *, final):
    rows = h.shape[0]
    layer_block = lambda arr: pl.BlockSpec((None,) + arr.shape[1:], lambda i: (layer,) + (0,) * (arr.ndim - 1))
    aliased = () if prev_states is None else tuple(prev_states)
    n_in = 5 + len(weights)
    return pl.pallas_call(
        functools.partial(_sample_kernel, pos0=PAST_LEN, n_aliased=len(aliased), final=final),
        grid=(1,),
        in_specs=[_resident(), layer_block(p_all), layer_block(st_a), layer_block(st_p), layer_block(st_c)]
                 + [_layer_resident(w, layer) for w in weights] + [_untouched()] * len(aliased),
        out_specs=[_resident(), layer_block(st_a), layer_block(st_p), layer_block(st_c)],
        out_shape=[jax.ShapeDtypeStruct(h.shape, F32),
                   jax.ShapeDtypeStruct(st_a.shape, F32),
                   jax.ShapeDtypeStruct(st_p.shape, F32),
                   jax.ShapeDtypeStruct(st_c.shape, F32)],
        input_output_aliases={n_in + k: 1 + k for k in range(len(aliased))},
        scratch_shapes=[pltpu.VMEM((rows, D_MODEL), BF16)],
        compiler_params=pltpu.CompilerParams(
            dimension_semantics=("arbitrary",),
            vmem_limit_bytes=VMEM_LIMIT_BYTES),
        name="sample_layer",
    )(h, p_all, st_a, st_p, st_c, *weights, *aliased)


def _pool_block_diag(pool_w):
    depth, groups, c, _ = pool_w.shape
    eye = jnp.eye(groups, dtype=pool_w.dtype)
    return (eye[None, :, None, :, None] * pool_w[:, :, :, None, :]).reshape(depth, groups * c, groups * c)


def _history_major(x):
    return jnp.transpose(x, (0, 2, 1, 3))


def kernel(x_prompt, x_sample, state_conv_a, state_pool, state_conv_c, p_prompt, p_sample, norm_mix_g, w_in, conv_a_w, conv_a_b, ln_a_g, ln_a_b, pool_w, pool_scale, conv_c_w, w_out, norm_mlp_g, w_up, w_down, norm_ple_g, w_ple_gate, w_ple_proj, final_norm_g):
    depth = w_in.shape[0]
    dec_batch, dec_seq, _ = x_sample.shape
    assert dec_seq == 1
    hp = x_prompt
    hs = x_sample.reshape(dec_batch, D_MODEL)
    ps = p_sample.reshape(depth, dec_batch, D_PLE)
    sa, sp, sc = (_history_major(x) for x in (state_conv_a, state_pool, state_conv_c))
    row = lambda v: v.reshape(depth, 1, -1)
    weights = (row(norm_mix_g), w_in.astype(BF16), conv_a_w, row(conv_a_b), row(ln_a_g), row(ln_a_b),
               _pool_block_diag(pool_w).astype(BF16), row(pool_scale), conv_c_w, w_out.astype(BF16),
               row(norm_mlp_g), w_up.astype(BF16), w_down.astype(BF16), row(norm_ple_g),
               w_ple_gate.astype(BF16), w_ple_proj.astype(BF16),
               jnp.broadcast_to(final_norm_g.reshape(1, 1, -1), (depth, 1, D_MODEL)))
    prompt_states = sample_states = None
    for i in range(depth):
        final = i == depth - 1
        hp, *prompt_states = _prompt_layer(hp, p_prompt, i, weights, prompt_states, final=final)
        hs, *sample_states = _sample_layer(hs, ps, sa, sp, sc, i, weights, sample_states, final=final)
    return (hp, hs.reshape(x_sample.shape), *prompt_states,
            *(_history_major(x) for x in sample_states))
```

```python
import functools

import jax
import jax.numpy as jnp
from jax import lax
from jax.experimental import pallas as pl
from jax.experimental.pallas import tpu as pltpu

D_MODEL = 1024
D_A = 384
D_B = 256
D_C = 384
D_IN = 2 * D_A + D_B + 3 * D_C
D_FF = 4096
D_PLE = 256
POOL_WINDOWS = (2, 4, 8, 16)
POOL_GROUP_DIM = D_B // len(POOL_WINDOWS)
POOL_STATE = 15
CONV_A_WIDTH = 31
CONV_C_WIDTH = 3
N_HIST_A = CONV_A_WIDTH - 1
N_HIST_C = CONV_C_WIDTH - 1
PAST_LEN = 16384
EPS = 1e-6

_O_AV = 0
_O_AG = D_A
_O_UB = 2 * D_A
_O_CB = _O_UB + D_B
_O_CC = _O_CB + D_C
_O_CX = _O_CC + D_C
_M_A = 0
_M_B = D_A
_M_C = D_A + D_B

LANES = 128
HIST_A = 32
HIST_P = 16
HIST_C = 8
ROW_CHUNK = 64
SAMPLE_ROW_CHUNK = 32
FF_SLICE = 1024
SEQ_TILE = 512
VMEM_LIMIT_BYTES = 62 * 1024 * 1024

BF16 = jnp.bfloat16
F32 = jnp.float32


def _rms_norm(x, g):
    ms = jnp.mean(x * x, axis=-1, keepdims=True)
    return x * lax.rsqrt(ms + EPS) * g


def _dot(a, b):
    return jnp.dot(a, b, preferred_element_type=F32)


def _layer_norm_silu(ya, g, b):
    mu = jnp.mean(ya, axis=-1, keepdims=True)
    xc = ya - mu
    var = jnp.mean(xc * xc, axis=-1, keepdims=True)
    y = xc * lax.rsqrt(var + EPS) * g + b
    return y * jax.nn.sigmoid(y)


def _pool_select(s_small, s_big, w_small, w_big, pos1):
    lane = lax.broadcasted_iota(jnp.int32, s_small.shape, 1)
    cnt_small = jnp.minimum(pos1, w_small).astype(F32)
    cnt_big = jnp.minimum(pos1, w_big).astype(F32)
    return jnp.where(lane < POOL_GROUP_DIM, s_small / cnt_small, s_big / cnt_big)


def _lane_cols(offset, c):
    return slice(offset + c * LANES, offset + (c + 1) * LANES)


def _mlp(h1, g_mlp, w_up, w_down, interleave=None):
    m = _rms_norm(h1, g_mlp[...]).astype(BF16)

    def up_act(j):
        up = _dot(m, w_up[:, j * FF_SLICE:(j + 1) * FF_SLICE])
        return jnp.square(jnp.maximum(up, 0.0)).astype(BF16)

    n_ff = D_FF // FF_SLICE
    h2 = h1
    act = up_act(0)
    for j in range(n_ff):
        if interleave is not None:
            interleave(j)
        down = _dot(act, w_down[j * FF_SLICE:(j + 1) * FF_SLICE, :])
        if j + 1 < n_ff:
            act = up_act(j + 1)
        h2 = h2 + down
    return h2


def _gated_embedding(h2, p, g_ple, w_pg, w_pp, g_fin, final):
    r3 = _rms_norm(h2, g_ple[...]).astype(BF16)
    gate = jax.nn.sigmoid(_dot(r3, w_pg[...]))
    h3 = h2 + gate * _dot(p.astype(BF16), w_pp[...])
    if final:
        h3 = _rms_norm(h3, g_fin[...])
    return h3


def _prompt_kernel(hn_ref, hb_ref, p_ref, g_mix, w_in, caw, cab, lng, lnb, pool_bd, pscale, ccw,
                   w_out, g_mlp, w_up, w_down, g_ple, w_pg, w_pp, g_fin, *rest,
                   tile, n_tiles, tiles_per_seq, n_aliased, final):
    (out_ref, na_ref, np_ref, nc_ref,
     zbuf, gbuf, ubuf, vbuf, poolbuf, mixbuf, nbuf, h2buf, mbuf, actbuf) = rest[n_aliased:]
    s = pl.program_id(0)
    t = s % tiles_per_seq

    def shifted(buf, c, start):
        return buf[c, pl.ds(start, ROW_CHUNK, stride=1), :]

    def mixers(r0):
        rows = pl.ds(r0, ROW_CHUNK)
        for c in range(D_A // LANES):
            glu = zbuf[rows, _lane_cols(_O_AV, c)] * jax.nn.sigmoid(zbuf[rows, _lane_cols(_O_AG, c)])
            gbuf[c, pl.ds(r0 + HIST_A, ROW_CHUNK), :] = glu
        for c in range(D_B // LANES):
            ubuf[c, pl.ds(r0 + HIST_P, ROW_CHUNK), :] = zbuf[rows, _lane_cols(_O_UB, c)]
        for c in range(D_C // LANES):
            vbuf[c, pl.ds(r0 + HIST_C, ROW_CHUNK), :] = (
                zbuf[rows, _lane_cols(_O_CC, c)] * zbuf[rows, _lane_cols(_O_CX, c)])

        conv_a = []
        for c in range(D_A // LANES):
            acc = jnp.broadcast_to(cab[:, _lane_cols(0, c)], (ROW_CHUNK, LANES))
            for k in range(CONV_A_WIDTH):
                acc = acc + caw[k:k + 1, _lane_cols(0, c)] * shifted(gbuf, c, r0 + HIST_A - N_HIST_A + k)
            conv_a.append(acc)
        out_a = _layer_norm_silu(jnp.concatenate(conv_a, axis=-1), lng[...], lnb[...])
        mixbuf[rows, _M_A:_M_A + D_A] = out_a.astype(BF16)

        pos1 = (t * tile + r0 + 1) + lax.broadcasted_iota(jnp.int32, (ROW_CHUNK, LANES), 0)
        u_lo = [shifted(ubuf, 0, r0 + HIST_P - j) for j in range(4)]
        s2 = u_lo[0] + u_lo[1]
        s4 = s2 + (u_lo[2] + u_lo[3])
        pooled_lo = _pool_select(s2, s4, 2, 4, pos1) - u_lo[0]
        u_hi = [shifted(ubuf, 1, r0 + HIST_P - j) for j in range(16)]
        s8 = u_hi[0]
        for j in range(1, 8):
            s8 = s8 + u_hi[j]
        s16 = s8
        for j in range(8, 16):
            s16 = s16 + u_hi[j]
        pooled_hi = _pool_select(s8, s16, 8, 16, pos1) - u_hi[0]
        poolbuf[rows, 0:LANES] = pooled_lo.astype(BF16)
        poolbuf[rows, LANES:2 * LANES] = pooled_hi.astype(BF16)

        for c in range(D_C // LANES):
            conv_c = ccw[2:3, _lane_cols(0, c)] * shifted(vbuf, c, r0 + HIST_C)
            conv_c = conv_c + ccw[1:2, _lane_cols(0, c)] * shifted(vbuf, c, r0 + HIST_C - 1)
            conv_c = conv_c + ccw[0:1, _lane_cols(0, c)] * shifted(vbuf, c, r0 + HIST_C - 2)
            out_c = zbuf[rows, _lane_cols(_O_CB, c)] * conv_c
            mixbuf[rows, _lane_cols(_M_C, c)] = out_c.astype(BF16)

    n_ff = D_FF // FF_SLICE
    chunks_per_slice = tile // ROW_CHUNK // n_ff

    def mixer_blocks(j):
        for q in range(chunks_per_slice):
            mixers((j * chunks_per_slice + q) * ROW_CHUNK)

    def up_act(j):
        up = _dot(mbuf[...], w_up[:, j * FF_SLICE:(j + 1) * FF_SLICE])
        actbuf[j % 2] = jnp.square(jnp.maximum(up, 0.0)).astype(BF16)

    def step(front, back):
        if front:
            @pl.when(t == 0)
            def _():
                gbuf[:, 0:HIST_A, :] = jnp.zeros((D_A // LANES, HIST_A, LANES), F32)
                ubuf[:, 0:HIST_P, :] = jnp.zeros((D_B // LANES, HIST_P, LANES), F32)
                vbuf[:, 0:HIST_C, :] = jnp.zeros((D_C // LANES, HIST_C, LANES), F32)

        if front and not back:
            nbuf[...] = _rms_norm(hb_ref[...], g_mix[...]).astype(BF16)
        if back:
            h1 = hb_ref[...] + _dot(mixbuf[...], w_out[...])
        if front:
            zbuf[...] = _dot(nbuf[...], w_in[...])
            nbuf[...] = _rms_norm(hn_ref[...], g_mix[...]).astype(BF16)
        if back:
            h2buf[...] = h1
            mbuf[...] = _rms_norm(h1, g_mlp[...]).astype(BF16)
            up_act(0)

        for j in range(n_ff):
            if front:
                mixer_blocks(j)
            if back:
                down = _dot(actbuf[j % 2], w_down[j * FF_SLICE:(j + 1) * FF_SLICE, :])
                if j + 1 < n_ff:
                    up_act(j + 1)
                h2buf[...] += down

        if front:
            out_b = _dot(poolbuf[...], pool_bd[...]) * pscale[...]
            mixbuf[:, _M_B:_M_B + D_B] = out_b.astype(BF16)
            gbuf[:, 0:HIST_A, :] = gbuf[:, tile:tile + HIST_A, :]
            ubuf[:, 0:HIST_P, :] = ubuf[:, tile:tile + HIST_P, :]
            vbuf[:, 0:HIST_C, :] = vbuf[:, tile:tile + HIST_C, :]
        if back:
            out_ref[...] = _gated_embedding(h2buf[...], p_ref[...], g_ple, w_pg, w_pp, g_fin, final)

        if front:
            @pl.when(t == tiles_per_seq - 1)
            def _():
                for c in range(D_A // LANES):
                    na_ref[:, _lane_cols(0, c)] = gbuf[c, HIST_A + tile - N_HIST_A:HIST_A + tile, :]
                for c in range(D_B // LANES):
                    np_ref[:, _lane_cols(0, c)] = ubuf[c, HIST_P + tile - POOL_STATE:HIST_P + tile, :]
                for c in range(D_C // LANES):
                    nc_ref[:, _lane_cols(0, c)] = vbuf[c, HIST_C + tile - N_HIST_C:HIST_C + tile, :]

    pl.when(s == 0)(lambda: step(True, False))
    pl.when((s > 0) & (s < n_tiles))(lambda: step(True, True))
    pl.when(s == n_tiles)(lambda: step(False, True))


def _sample_kernel(h_ref, p_ref, sa_ref, sp_ref, sc_ref, g_mix, w_in, caw, cab, lng, lnb,
                   pool_bd, pscale, ccw, w_out, g_mlp, w_up, w_down, g_ple, w_pg, w_pp, g_fin, *rest,
                   pos0, n_aliased, final):
    out_ref, na_ref, np_ref, nc_ref, mixbuf = rest[n_aliased:]
    h = h_ref[...]
    rows = h.shape[0]
    z = _dot(_rms_norm(h, g_mix[...]).astype(BF16), w_in[...])
    u = z[:, _O_UB:_O_UB + D_B]
    v = z[:, _O_CC:_O_CC + D_C] * z[:, _O_CX:_O_CX + D_C]
    glu = z[:, _O_AV:_O_AV + D_A] * jax.nn.sigmoid(z[:, _O_AG:_O_AG + D_A])

    for r0 in range(0, rows, SAMPLE_ROW_CHUNK):
        blk = slice(r0, r0 + SAMPLE_ROW_CHUNK)
        acc = cab[...] + caw[N_HIST_A:N_HIST_A + 1, :] * glu[blk]
        for k in range(N_HIST_A):
            acc = acc + caw[k:k + 1, :] * sa_ref[k, blk, :]
        mixbuf[blk, _M_A:_M_A + D_A] = _layer_norm_silu(acc, lng[...], lnb[...]).astype(BF16)
    na_ref[0:N_HIST_A - 1] = sa_ref[1:N_HIST_A]
    na_ref[N_HIST_A - 1] = glu

    pos1 = jnp.full((rows, LANES), pos0 + 1, jnp.int32)
    u_lo = u[:, 0:LANES]
    s2 = u_lo + sp_ref[POOL_STATE - 1, :, 0:LANES]
    s4 = s2 + (sp_ref[POOL_STATE - 2, :, 0:LANES] + sp_ref[POOL_STATE - 3, :, 0:LANES])
    pooled_lo = _pool_select(s2, s4, 2, 4, pos1) - u_lo
    u_hi = u[:, LANES:2 * LANES]
    s8 = u_hi
    for j in range(1, 8):
        s8 = s8 + sp_ref[POOL_STATE - j, :, LANES:2 * LANES]
    s16 = s8
    for j in range(8, 16):
        s16 = s16 + sp_ref[POOL_STATE - j, :, LANES:2 * LANES]
    pooled_hi = _pool_select(s8, s16, 8, 16, pos1) - u_hi
    pooled = jnp.concatenate([pooled_lo, pooled_hi], axis=-1).astype(BF16)
    mixbuf[:, _M_B:_M_B + D_B] = (_dot(pooled, pool_bd[...]) * pscale[...]).astype(BF16)
    np_ref[0:POOL_STATE - 1] = sp_ref[1:POOL_STATE]
    np_ref[POOL_STATE - 1] = u

    conv_c = ccw[2:3, :] * v + ccw[1:2, :] * sc_ref[1] + ccw[0:1, :] * sc_ref[0]
    mixbuf[:, _M_C:_M_C + D_C] = (z[:, _O_CB:_O_CB + D_C] * conv_c).astype(BF16)
    nc_ref[0] = sc_ref[1]
    nc_ref[1] = v

    h1 = h + _dot(mixbuf[...], w_out[...])
    h2 = _mlp(h1, g_mlp, w_up, w_down)
    out_ref[...] = _gated_embedding(h2, p_ref[...], g_ple, w_pg, w_pp, g_fin, final)


def _resident():
    return pl.BlockSpec(memory_space=pltpu.VMEM)


def _layer_resident(arr, layer):
    zeros = (0,) * (arr.ndim - 1)
    return pl.BlockSpec((None,) + arr.shape[1:], lambda s: (layer,) + zeros, pipeline_mode=pl.Buffered(1))


def _untouched():
    return pl.BlockSpec(memory_space=pl.ANY)


def _prompt_layer(h, p_all, layer, weights, prev_states, *, final):
    depth = p_all.shape[0]
    batch, seq, _ = h.shape
    tile = min(SEQ_TILE, seq)
    assert seq % tile == 0 and tile % (ROW_CHUNK * (D_FF // FF_SLICE)) == 0 and tile >= HIST_A
    tiles_per_seq = seq // tile
    n_tiles = batch * tiles_per_seq

    def front(s):
        i = jnp.minimum(s, n_tiles - 1)
        return i // tiles_per_seq, i % tiles_per_seq

    def back(s):
        i = jnp.maximum(s - 1, 0)
        return i // tiles_per_seq, i % tiles_per_seq

    def ahead(s):
        i = jnp.minimum(s + 1, n_tiles - 1)
        return i // tiles_per_seq, i % tiles_per_seq

    h_ahead = pl.BlockSpec((None, tile, D_MODEL), lambda s: (*ahead(s), 0))
    h_back = pl.BlockSpec((None, tile, D_MODEL), lambda s: (*back(s), 0))
    p_in = pl.BlockSpec((None, None, tile, D_PLE), lambda s: (layer, *back(s), 0))
    h_out = pl.BlockSpec((None, tile, D_MODEL), lambda s: (*back(s), 0))
    state_block = lambda rows, width: pl.BlockSpec(
        (None, None, rows, width), lambda s: (layer, front(s)[0], 0, 0))
    aliased = () if prev_states is None else tuple(prev_states)
    n_in = 3 + len(weights)
    return pl.pallas_call(
        functools.partial(_prompt_kernel, tile=tile, n_tiles=n_tiles, tiles_per_seq=tiles_per_seq,
                          n_aliased=len(aliased), final=final),
        grid=(n_tiles + 1,),
        in_specs=([h_ahead, h_back, p_in] + [_layer_resident(w, layer) for w in weights]
                  + [_untouched()] * len(aliased)),
        out_specs=[h_out, state_block(N_HIST_A, D_A),
                   state_block(POOL_STATE, D_B), state_block(N_HIST_C, D_C)],
        out_shape=[jax.ShapeDtypeStruct(h.shape, F32),
                   jax.ShapeDtypeStruct((depth, batch, N_HIST_A, D_A), F32),
                   jax.ShapeDtypeStruct((depth, batch, POOL_STATE, D_B), F32),
                   jax.ShapeDtypeStruct((depth, batch, N_HIST_C, D_C), F32)],
        input_output_aliases={n_in + k: 1 + k for k in range(len(aliased))},
        scratch_shapes=[pltpu.VMEM((tile, D_IN), F32),
                        pltpu.VMEM((D_A // LANES, HIST_A + tile, LANES), F32),
                        pltpu.VMEM((D_B // LANES, HIST_P + tile, LANES), F32),
                        pltpu.VMEM((D_C // LANES, HIST_C + tile, LANES), F32),
                        pltpu.VMEM((tile, D_B), BF16),
                        pltpu.VMEM((tile, D_MODEL), BF16),
                        pltpu.VMEM((tile, D_MODEL), BF16),
                        pltpu.VMEM((tile, D_MODEL), F32),
                        pltpu.VMEM((tile, D_MODEL), BF16),
                        pltpu.VMEM((2, tile, FF_SLICE), BF16)],
        compiler_params=pltpu.CompilerParams(
            dimension_semantics=("arbitrary",),
            vmem_limit_bytes=VMEM_LIMIT_BYTES),
        name="prompt_layer",
    )(h, h, p_all, *weights, *aliased)


def _sample_layer(h, p_all, st_a, st_p, st_c, layer, weights, prev_states, *, final):
    rows = h.shape[0]
    layer_block = lambda arr: pl.BlockSpec((None,) + arr.shape[1:], lambda i: (layer,) + (0,) * (arr.ndim - 1))
    aliased = () if prev_states is None else tuple(prev_states)
    n_in = 5 + len(weights)
    return pl.pallas_call(
        functools.partial(_sample_kernel, pos0=PAST_LEN, n_aliased=len(aliased), final=final),
        grid=(1,),
        in_specs=[_resident(), layer_block(p_all), layer_block(st_a), layer_block(st_p), layer_block(st_c)]
                 + [_layer_resident(w, layer) for w in weights] + [_untouched()] * len(aliased),
        out_specs=[_resident(), layer_block(st_a), layer_block(st_p), layer_block(st_c)],
        out_shape=[jax.ShapeDtypeStruct(h.shape, F32),
                   jax.ShapeDtypeStruct(st_a.shape, F32),
                   jax.ShapeDtypeStruct(st_p.shape, F32),
                   jax.ShapeDtypeStruct(st_c.shape, F32)],
        input_output_aliases={n_in + k: 1 + k for k in range(len(aliased))},
        scratch_shapes=[pltpu.VMEM((rows, D_MODEL), BF16)],
        compiler_params=pltpu.CompilerParams(
            dimension_semantics=("arbitrary",),
            vmem_limit_bytes=VMEM_LIMIT_BYTES),
        name="sample_layer",
    )(h, p_all, st_a, st_p, st_c, *weights, *aliased)


def _pool_block_diag(pool_w):
    depth, groups, c, _ = pool_w.shape
    eye = jnp.eye(groups, dtype=pool_w.dtype)
    return (eye[None, :, None, :, None] * pool_w[:, :, :, None, :]).reshape(depth, groups * c, groups * c)


def _history_major(x):
    return jnp.transpose(x, (0, 2, 1, 3))


def kernel(x_prompt, x_sample, state_conv_a, state_pool, state_conv_c, p_prompt, p_sample, norm_mix_g, w_in, conv_a_w, conv_a_b, ln_a_g, ln_a_b, pool_w, pool_scale, conv_c_w, w_out, norm_mlp_g, w_up, w_down, norm_ple_g, w_ple_gate, w_ple_proj, final_norm_g):
    depth = w_in.shape[0]
    dec_batch, dec_seq, _ = x_sample.shape
    assert dec_seq == 1
    hp = x_prompt
    hs = x_sample.reshape(dec_batch, D_MODEL)
    ps = p_sample.reshape(depth, dec_batch, D_PLE)
    sa, sp, sc = (_history_major(x) for x in (state_conv_a, state_pool, state_conv_c))
    row = lambda v: v.reshape(depth, 1, -1)
    weights = (row(norm_mix_g), w_in.astype(BF16), conv_a_w, row(conv_a_b), row(ln_a_g), row(ln_a_b),
               _pool_block_diag(pool_w).astype(BF16), row(pool_scale), conv_c_w, w_out.astype(BF16),
               row(norm_mlp_g), w_up.astype(BF16), w_down.astype(BF16), row(norm_ple_g),
               w_ple_gate.astype(BF16), w_ple_proj.astype(BF16),
               jnp.broadcast_to(final_norm_g.reshape(1, 1, -1), (depth, 1, D_MODEL)))
    prompt_states = sample_states = None
    for i in range(depth):
        final = i == depth - 1
        hp, *prompt_states = _prompt_layer(hp, p_prompt, i, weights, prompt_states, final=final)
        hs, *sample_states = _sample_layer(hs, ps, sa, sp, sc, i, weights, sample_states, final=final)
    return (hp, hs.reshape(x_sample.shape), *prompt_states,
            *(_history_major(x) for x in sample_states))
```

```python
import functools

import jax
import jax.numpy as jnp
from jax import lax
from jax.experimental import pallas as pl
from jax.experimental.pallas import tpu as pltpu

D_MODEL = 1024
D_A = 384
D_B = 256
D_C = 384
D_IN = 2 * D_A + D_B + 3 * D_C
D_FF = 4096
D_PLE = 256
POOL_WINDOWS = (2, 4, 8, 16)
POOL_GROUP_DIM = D_B // len(POOL_WINDOWS)
POOL_STATE = 15
CONV_A_WIDTH = 31
CONV_C_WIDTH = 3
N_HIST_A = CONV_A_WIDTH - 1
N_HIST_C = CONV_C_WIDTH - 1
PAST_LEN = 16384
EPS = 1e-6

_O_AV = 0
_O_AG = D_A
_O_UB = 2 * D_A
_O_CB = _O_UB + D_B
_O_CC = _O_CB + D_C
_O_CX = _O_CC + D_C
_M_A = 0
_M_B = D_A
_M_C = D_A + D_B

LANES = 128
HIST_A = 32
HIST_P = 16
HIST_C = 8
ROW_CHUNK = 64
SAMPLE_ROW_CHUNK = 32
FF_SLICE = 1024
SEQ_TILE = 512
VMEM_LIMIT_BYTES = 62 * 1024 * 1024

BF16 = jnp.bfloat16
F32 = jnp.float32


def _rms_norm(x, g):
    ms = jnp.mean(x * x, axis=-1, keepdims=True)
    return x * lax.rsqrt(ms + EPS) * g


def _dot(a, b):
    return jnp.dot(a, b, preferred_element_type=F32)


def _layer_norm_silu(ya, g, b):
    mu = jnp.mean(ya, axis=-1, keepdims=True)
    xc = ya - mu
    var = jnp.mean(xc * xc, axis=-1, keepdims=True)
    y = xc * lax.rsqrt(var + EPS) * g + b
    return y * jax.nn.sigmoid(y)


def _pool_select(s_small, s_big, w_small, w_big, pos1):
    lane = lax.broadcasted_iota(jnp.int32, s_small.shape, 1)
    cnt_small = jnp.minimum(pos1, w_small).astype(F32)
    cnt_big = jnp.minimum(pos1, w_big).astype(F32)
    return jnp.where(lane < POOL_GROUP_DIM, s_small / cnt_small, s_big / cnt_big)


def _lane_cols(offset, c):
    return slice(offset + c * LANES, offset + (c + 1) * LANES)


def _mlp(h1, g_mlp, w_up, w_down):
    m = _rms_norm(h1, g_mlp[...]).astype(BF16)

    def up_act(j):
        up = _dot(m, w_up[:, j * FF_SLICE:(j + 1) * FF_SLICE])
        return jnp.square(jnp.maximum(up, 0.0)).astype(BF16)

    n_ff = D_FF // FF_SLICE
    h2 = h1
    act = up_act(0)
    for j in range(n_ff):
        down = _dot(act, w_down[j * FF_SLICE:(j + 1) * FF_SLICE, :])
        if j + 1 < n_ff:
            act = up_act(j + 1)
        h2 = h2 + down
    return h2


def _gated_embedding(h2, p, g_ple, w_pg, w_pp, g_fin, final):
    r3 = _rms_norm(h2, g_ple[...]).astype(BF16)
    gate = jax.nn.sigmoid(_dot(r3, w_pg[...]))
    h3 = h2 + gate * _dot(p.astype(BF16), w_pp[...])
    if final:
        h3 = _rms_norm(h3, g_fin[...])
    return h3


def _prompt_kernel(hn_ref, hb_ref, p_ref, g_mix, w_in, caw, cab, lng, lnb, pool_bd, pscale, ccw,
                   w_out, g_mlp, w_up, w_down, g_ple, w_pg, w_pp, g_fin, *rest,
                   tile, n_tiles, tiles_per_seq, final):
    (out_ref, na_ref, np_ref, nc_ref,
     zbuf, gbuf, ubuf, vbuf, poolbuf, mixbuf, nbuf, h2buf, mbuf, actbuf) = rest
    s = pl.program_id(0)
    t = s % tiles_per_seq

    def shifted(buf, c, start):
        return buf[c, pl.ds(start, ROW_CHUNK, stride=1), :]

    def mixers(r0):
        rows = pl.ds(r0, ROW_CHUNK)
        for c in range(D_A // LANES):
            glu = zbuf[rows, _lane_cols(_O_AV, c)] * jax.nn.sigmoid(zbuf[rows, _lane_cols(_O_AG, c)])
            gbuf[c, pl.ds(r0 + HIST_A, ROW_CHUNK), :] = glu
        for c in range(D_B // LANES):
            ubuf[c, pl.ds(r0 + HIST_P, ROW_CHUNK), :] = zbuf[rows, _lane_cols(_O_UB, c)]
        for c in range(D_C // LANES):
            vbuf[c, pl.ds(r0 + HIST_C, ROW_CHUNK), :] = (
                zbuf[rows, _lane_cols(_O_CC, c)] * zbuf[rows, _lane_cols(_O_CX, c)])

        conv_a = []
        for c in range(D_A // LANES):
            acc = jnp.broadcast_to(cab[:, _lane_cols(0, c)], (ROW_CHUNK, LANES))
            for k in range(CONV_A_WIDTH):
                acc = acc + caw[k:k + 1, _lane_cols(0, c)] * shifted(gbuf, c, r0 + HIST_A - N_HIST_A + k)
            conv_a.append(acc)
        out_a = _layer_norm_silu(jnp.concatenate(conv_a, axis=-1), lng[...], lnb[...])
        mixbuf[rows, _M_A:_M_A + D_A] = out_a.astype(BF16)

        pos1 = (t * tile + r0 + 1) + lax.broadcasted_iota(jnp.int32, (ROW_CHUNK, LANES), 0)
        u_lo = [shifted(ubuf, 0, r0 + HIST_P - j) for j in range(4)]
        s2 = u_lo[0] + u_lo[1]
        s4 = s2 + (u_lo[2] + u_lo[3])
        pooled_lo = _pool_select(s2, s4, 2, 4, pos1) - u_lo[0]
        u_hi = [shifted(ubuf, 1, r0 + HIST_P - j) for j in range(16)]
        s8 = u_hi[0]
        for j in range(1, 8):
            s8 = s8 + u_hi[j]
        s16 = s8
        for j in range(8, 16):
            s16 = s16 + u_hi[j]
        pooled_hi = _pool_select(s8, s16, 8, 16, pos1) - u_hi[0]
        poolbuf[rows, 0:LANES] = pooled_lo.astype(BF16)
        poolbuf[rows, LANES:2 * LANES] = pooled_hi.astype(BF16)

        for c in range(D_C // LANES):
            conv_c = ccw[2:3, _lane_cols(0, c)] * shifted(vbuf, c, r0 + HIST_C)
            conv_c = conv_c + ccw[1:2, _lane_cols(0, c)] * shifted(vbuf, c, r0 + HIST_C - 1)
            conv_c = conv_c + ccw[0:1, _lane_cols(0, c)] * shifted(vbuf, c, r0 + HIST_C - 2)
            out_c = zbuf[rows, _lane_cols(_O_CB, c)] * conv_c
            mixbuf[rows, _lane_cols(_M_C, c)] = out_c.astype(BF16)

    n_ff = D_FF // FF_SLICE
    chunks_per_slice = tile // ROW_CHUNK // n_ff

    def mixer_blocks(j):
        for q in range(chunks_per_slice):
            mixers((j * chunks_per_slice + q) * ROW_CHUNK)

    def up_act(j):
        up = _dot(mbuf[...], w_up[:, j * FF_SLICE:(j + 1) * FF_SLICE])
        actbuf[j % 2] = jnp.square(jnp.maximum(up, 0.0)).astype(BF16)

    def step(front, back):
        if front:
            @pl.when(t == 0)
            def _():
                gbuf[:, 0:HIST_A, :] = jnp.zeros((D_A // LANES, HIST_A, LANES), F32)
                ubuf[:, 0:HIST_P, :] = jnp.zeros((D_B // LANES, HIST_P, LANES), F32)
                vbuf[:, 0:HIST_C, :] = jnp.zeros((D_C // LANES, HIST_C, LANES), F32)

        if front and not back:
            nbuf[...] = _rms_norm(hb_ref[...], g_mix[...]).astype(BF16)
        if back:
            h1 = hb_ref[...] + _dot(mixbuf[...], w_out[...])
        if front:
            zbuf[...] = _dot(nbuf[...], w_in[...])
            nbuf[...] = _rms_norm(hn_ref[...], g_mix[...]).astype(BF16)
        if back:
            h2buf[...] = h1
            mbuf[...] = _rms_norm(h1, g_mlp[...]).astype(BF16)
            up_act(0)

        for j in range(n_ff):
            if front:
                mixer_blocks(j)
            if back:
                down = _dot(actbuf[j % 2], w_down[j * FF_SLICE:(j + 1) * FF_SLICE, :])
                if j + 1 < n_ff:
                    up_act(j + 1)
                h2buf[...] += down

        if front:
            out_b = _dot(poolbuf[...], pool_bd[...]) * pscale[...]
            mixbuf[:, _M_B:_M_B + D_B] = out_b.astype(BF16)
            gbuf[:, 0:HIST_A, :] = gbuf[:, tile:tile + HIST_A, :]
            ubuf[:, 0:HIST_P, :] = ubuf[:, tile:tile + HIST_P, :]
            vbuf[:, 0:HIST_C, :] = vbuf[:, tile:tile + HIST_C, :]
        if back:
            out_ref[...] = _gated_embedding(h2buf[...], p_ref[...], g_ple, w_pg, w_pp, g_fin, final)

        if front:
            @pl.when(t == tiles_per_seq - 1)
            def _():
                for c in range(D_A // LANES):
                    na_ref[:, _lane_cols(0, c)] = gbuf[c, HIST_A + tile - N_HIST_A:HIST_A + tile, :]
                for c in range(D_B // LANES):
                    np_ref[:, _lane_cols(0, c)] = ubuf[c, HIST_P + tile - POOL_STATE:HIST_P + tile, :]
                for c in range(D_C // LANES):
                    nc_ref[:, _lane_cols(0, c)] = vbuf[c, HIST_C + tile - N_HIST_C:HIST_C + tile, :]

    pl.when(s == 0)(lambda: step(True, False))
    pl.when((s > 0) & (s < n_tiles))(lambda: step(True, True))
    pl.when(s == n_tiles)(lambda: step(False, True))


def _sample_kernel(h_ref, p_ref, sa_ref, sp_ref, sc_ref, g_mix, w_in, caw, cab, lng, lnb,
                   pool_bd, pscale, ccw, w_out, g_mlp, w_up, w_down, g_ple, w_pg, w_pp, g_fin, *rest,
                   pos0):
    out_ref, na_ref, np_ref, nc_ref, mixbuf, hbuf = rest
    layer = pl.program_id(0)

    @pl.when(layer == 0)
    def _():
        hbuf[...] = h_ref[...]

    h = hbuf[...]
    rows = h.shape[0]
    z = _dot(_rms_norm(h, g_mix[...]).astype(BF16), w_in[...])
    u = z[:, _O_UB:_O_UB + D_B]
    v = z[:, _O_CC:_O_CC + D_C] * z[:, _O_CX:_O_CX + D_C]
    glu = z[:, _O_AV:_O_AV + D_A] * jax.nn.sigmoid(z[:, _O_AG:_O_AG + D_A])

    for r0 in range(0, rows, SAMPLE_ROW_CHUNK):
        blk = slice(r0, r0 + SAMPLE_ROW_CHUNK)
        acc = cab[...] + caw[N_HIST_A:N_HIST_A + 1, :] * glu[blk]
        for k in range(N_HIST_A):
            acc = acc + caw[k:k + 1, :] * sa_ref[k, blk, :]
        mixbuf[blk, _M_A:_M_A + D_A] = _layer_norm_silu(acc, lng[...], lnb[...]).astype(BF16)
    na_ref[0:N_HIST_A - 1] = sa_ref[1:N_HIST_A]
    na_ref[N_HIST_A - 1] = glu

    pos1 = jnp.full((rows, LANES), pos0 + 1, jnp.int32)
    u_lo = u[:, 0:LANES]
    s2 = u_lo + sp_ref[POOL_STATE - 1, :, 0:LANES]
    s4 = s2 + (sp_ref[POOL_STATE - 2, :, 0:LANES] + sp_ref[POOL_STATE - 3, :, 0:LANES])
    pooled_lo = _pool_select(s2, s4, 2, 4, pos1) - u_lo
    u_hi = u[:, LANES:2 * LANES]
    s8 = u_hi
    for j in range(1, 8):
        s8 = s8 + sp_ref[POOL_STATE - j, :, LANES:2 * LANES]
    s16 = s8
    for j in range(8, 16):
        s16 = s16 + sp_ref[POOL_STATE - j, :, LANES:2 * LANES]
    pooled_hi = _pool_select(s8, s16, 8, 16, pos1) - u_hi
    pooled = jnp.concatenate([pooled_lo, pooled_hi], axis=-1).astype(BF16)
    mixbuf[:, _M_B:_M_B + D_B] = (_dot(pooled, pool_bd[...]) * pscale[...]).astype(BF16)
    np_ref[0:POOL_STATE - 1] = sp_ref[1:POOL_STATE]
    np_ref[POOL_STATE - 1] = u

    conv_c = ccw[2:3, :] * v + ccw[1:2, :] * sc_ref[1] + ccw[0:1, :] * sc_ref[0]
    mixbuf[:, _M_C:_M_C + D_C] = (z[:, _O_CB:_O_CB + D_C] * conv_c).astype(BF16)
    nc_ref[0] = sc_ref[1]
    nc_ref[1] = v

    h1 = h + _dot(mixbuf[...], w_out[...])
    h2 = _mlp(h1, g_mlp, w_up, w_down)
    h3 = _gated_embedding(h2, p_ref[...], g_ple, w_pg, w_pp, g_fin, False)
    hbuf[...] = h3

    @pl.when(layer == pl.num_programs(0) - 1)
    def _():
        out_ref[...] = _rms_norm(h3, g_fin[...])


def _resident():
    return pl.BlockSpec(memory_space=pltpu.VMEM)


def _layer_resident(arr, layer):
    zeros = (0,) * (arr.ndim - 1)
    index_map = (lambda s: (s,) + zeros) if layer is None else (lambda s: (layer,) + zeros)
    return pl.BlockSpec((None,) + arr.shape[1:], index_map, pipeline_mode=pl.Buffered(1))


def _prompt_layer(h, p_all, layer, weights, *, final):
    batch, seq, _ = h.shape
    tile = min(SEQ_TILE, seq)
    assert seq % tile == 0 and tile % (ROW_CHUNK * (D_FF // FF_SLICE)) == 0 and tile >= HIST_A
    tiles_per_seq = seq // tile
    n_tiles = batch * tiles_per_seq

    def front(s):
        i = jnp.minimum(s, n_tiles - 1)
        return i // tiles_per_seq, i % tiles_per_seq

    def back(s):
        i = jnp.maximum(s - 1, 0)
        return i // tiles_per_seq, i % tiles_per_seq

    def ahead(s):
        i = jnp.minimum(s + 1, n_tiles - 1)
        return i // tiles_per_seq, i % tiles_per_seq

    h_ahead = pl.BlockSpec((None, tile, D_MODEL), lambda s: (*ahead(s), 0))
    h_back = pl.BlockSpec((None, tile, D_MODEL), lambda s: (*back(s), 0))
    p_in = pl.BlockSpec((None, None, tile, D_PLE), lambda s: (layer, *back(s), 0))
    h_out = pl.BlockSpec((None, tile, D_MODEL), lambda s: (*back(s), 0))
    state_block = lambda rows, width: pl.BlockSpec((None, rows, width), lambda s: (front(s)[0], 0, 0))
    return pl.pallas_call(
        functools.partial(_prompt_kernel, tile=tile, n_tiles=n_tiles, tiles_per_seq=tiles_per_seq,
                          final=final),
        grid=(n_tiles + 1,),
        in_specs=[h_ahead, h_back, p_in] + [_layer_resident(w, layer) for w in weights],
        out_specs=[h_out, state_block(N_HIST_A, D_A),
                   state_block(POOL_STATE, D_B), state_block(N_HIST_C, D_C)],
        out_shape=[jax.ShapeDtypeStruct(h.shape, F32),
                   jax.ShapeDtypeStruct((batch, N_HIST_A, D_A), F32),
                   jax.ShapeDtypeStruct((batch, POOL_STATE, D_B), F32),
                   jax.ShapeDtypeStruct((batch, N_HIST_C, D_C), F32)],
        scratch_shapes=[pltpu.VMEM((tile, D_IN), F32),
                        pltpu.VMEM((D_A // LANES, HIST_A + tile, LANES), F32),
                        pltpu.VMEM((D_B // LANES, HIST_P + tile, LANES), F32),
                        pltpu.VMEM((D_C // LANES, HIST_C + tile, LANES), F32),
                        pltpu.VMEM((tile, D_B), BF16),
                        pltpu.VMEM((tile, D_MODEL), BF16),
                        pltpu.VMEM((tile, D_MODEL), BF16),
                        pltpu.VMEM((tile, D_MODEL), F32),
                        pltpu.VMEM((tile, D_MODEL), BF16),
                        pltpu.VMEM((2, tile, FF_SLICE), BF16)],
        compiler_params=pltpu.CompilerParams(
            dimension_semantics=("arbitrary",),
            vmem_limit_bytes=VMEM_LIMIT_BYTES),
        name="prompt_layer",
    )(h, h, p_all, *weights)


def _sample_layers(h, p_all, st_a, st_p, st_c, weights):
    depth = p_all.shape[0]
    rows = h.shape[0]
    per_layer = lambda arr: _layer_resident(arr, None)
    layer_out = lambda arr: pl.BlockSpec((None,) + arr.shape[1:], lambda s: (s,) + (0,) * (arr.ndim - 1))
    return pl.pallas_call(
        functools.partial(_sample_kernel, pos0=PAST_LEN),
        grid=(depth,),
        in_specs=[_resident(), per_layer(p_all), per_layer(st_a), per_layer(st_p), per_layer(st_c)]
                 + [per_layer(w) for w in weights],
        out_specs=[_resident(), layer_out(st_a), layer_out(st_p), layer_out(st_c)],
        out_shape=[jax.ShapeDtypeStruct(h.shape, F32),
                   jax.ShapeDtypeStruct(st_a.shape, F32),
                   jax.ShapeDtypeStruct(st_p.shape, F32),
                   jax.ShapeDtypeStruct(st_c.shape, F32)],
        scratch_shapes=[pltpu.VMEM((rows, D_MODEL), BF16), pltpu.VMEM((rows, D_MODEL), F32)],
        compiler_params=pltpu.CompilerParams(
            dimension_semantics=("arbitrary",),
            vmem_limit_bytes=VMEM_LIMIT_BYTES),
        name="sample_layers",
    )(h, p_all, st_a, st_p, st_c, *weights)


def _pool_block_diag(pool_w):
    depth, groups, c, _ = pool_w.shape
    eye = jnp.eye(groups, dtype=pool_w.dtype)
    return (eye[None, :, None, :, None] * pool_w[:, :, :, None, :]).reshape(depth, groups * c, groups * c)


def _history_major(x):
    return jnp.transpose(x, (0, 2, 1, 3))


def kernel(x_prompt, x_sample, state_conv_a, state_pool, state_conv_c, p_prompt, p_sample, norm_mix_g, w_in, conv_a_w, conv_a_b, ln_a_g, ln_a_b, pool_w, pool_scale, conv_c_w, w_out, norm_mlp_g, w_up, w_down, norm_ple_g, w_ple_gate, w_ple_proj, final_norm_g):
    depth = w_in.shape[0]
    dec_batch, dec_seq, _ = x_sample.shape
    assert dec_seq == 1
    hp = x_prompt
    hs = x_sample.reshape(dec_batch, D_MODEL)
    ps = p_sample.reshape(depth, dec_batch, D_PLE)
    sa, sp, sc = (_history_major(x) for x in (state_conv_a, state_pool, state_conv_c))
    row = lambda v: v.reshape(depth, 1, -1)
    weights = (row(norm_mix_g), w_in.astype(BF16), conv_a_w, row(conv_a_b), row(ln_a_g), row(ln_a_b),
               _pool_block_diag(pool_w).astype(BF16), row(pool_scale), conv_c_w, w_out.astype(BF16),
               row(norm_mlp_g), w_up.astype(BF16), w_down.astype(BF16), row(norm_ple_g),
               w_ple_gate.astype(BF16), w_ple_proj.astype(BF16),
               jnp.broadcast_to(final_norm_g.reshape(1, 1, -1), (depth, 1, D_MODEL)))
    prompt_states = []
    for i in range(depth):
        hp, *states = _prompt_layer(hp, p_prompt, i, weights, final=i == depth - 1)
        prompt_states.append(states)
    hs, *sample_states = _sample_layers(hs, ps, sa, sp, sc, weights)
    return (hp, hs.reshape(x_sample.shape), *(jnp.stack(x) for x in zip(*prompt_states)),
            *(_history_major(x) for x in sample_states))
```

```python
import functools

import jax
import jax.numpy as jnp
from jax import lax
from jax.experimental import pallas as pl
from jax.experimental.pallas import tpu as pltpu

D_MODEL = 1024
D_A = 384
D_B = 256
D_C = 384
D_IN = 2 * D_A + D_B + 3 * D_C
D_FF = 4096
D_PLE = 256
POOL_WINDOWS = (2, 4, 8, 16)
POOL_GROUP_DIM = D_B // len(POOL_WINDOWS)
POOL_STATE = 15
CONV_A_WIDTH = 31
CONV_C_WIDTH = 3
N_HIST_A = CONV_A_WIDTH - 1
N_HIST_C = CONV_C_WIDTH - 1
PAST_LEN = 16384
EPS = 1e-6

_O_AV = 0
_O_AG = D_A
_O_UB = 2 * D_A
_O_CB = _O_UB + D_B
_O_CC = _O_CB + D_C
_O_CX = _O_CC + D_C
_M_A = 0
_M_B = D_A
_M_C = D_A + D_B

LANES = 128
HIST_A = 32
HIST_P = 16
HIST_C = 8
ROW_CHUNK = 64
SAMPLE_ROW_CHUNK = 32
FF_SLICE = 1024
SEQ_TILE = 512
VMEM_LIMIT_BYTES = 62 * 1024 * 1024

BF16 = jnp.bfloat16
F32 = jnp.float32


_VEC_SIZES = (("g_mix", D_MODEL), ("g_mlp", D_MODEL), ("g_ple", D_MODEL), ("g_fin", D_MODEL),
              ("cab", D_A), ("lng", D_A), ("lnb", D_A), ("pscale", D_B))
_VEC_LAYOUT = {}
_VEC_WIDTH = 0
for _name, _size in _VEC_SIZES:
    _VEC_LAYOUT[_name] = (_VEC_WIDTH, _size)
    _VEC_WIDTH += _size


class _Cols:
    def __init__(self, ref, start, size):
        self.ref, self.start, self.size = ref, start, size

    def __getitem__(self, idx):
        if idx is Ellipsis:
            return self.ref[:, self.start:self.start + self.size]
        rows, cols = idx
        lo = cols.start or 0
        hi = self.size if cols.stop is None else cols.stop
        return self.ref[rows, self.start + lo:self.start + hi]


def _vector_params(vecs):
    return tuple(_Cols(vecs, *_VEC_LAYOUT[name]) for name, _ in _VEC_SIZES)


def _rms_norm(x, g):
    ms = jnp.mean(x * x, axis=-1, keepdims=True)
    return x * lax.rsqrt(ms + EPS) * g


def _dot(a, b):
    return jnp.dot(a, b, preferred_element_type=F32)


def _layer_norm_silu(ya, g, b):
    mu = jnp.mean(ya, axis=-1, keepdims=True)
    xc = ya - mu
    var = jnp.mean(xc * xc, axis=-1, keepdims=True)
    y = xc * lax.rsqrt(var + EPS) * g + b
    return y * jax.nn.sigmoid(y)


def _pool_select(s_small, s_big, w_small, w_big, pos1):
    lane = lax.broadcasted_iota(jnp.int32, s_small.shape, 1)
    cnt_small = jnp.minimum(pos1, w_small).astype(F32)
    cnt_big = jnp.minimum(pos1, w_big).astype(F32)
    return jnp.where(lane < POOL_GROUP_DIM, s_small / cnt_small, s_big / cnt_big)


def _lane_cols(offset, c):
    return slice(offset + c * LANES, offset + (c + 1) * LANES)


def _gated_embedding(h2, p, g_ple, w_pg, w_pp, g_fin, final):
    r3 = _rms_norm(h2, g_ple[...]).astype(BF16)
    gate = jax.nn.sigmoid(_dot(r3, w_pg[...]))
    h3 = h2 + gate * _dot(p.astype(BF16), w_pp[...])
    if final:
        h3 = _rms_norm(h3, g_fin[...])
    return h3


def _prompt_kernel(hn_ref, hb_ref, p_ref, vecs, w_in, caw, pool_bd, ccw, w_out, w_up, w_down, w_pg, w_pp, *rest,
                   tile, n_tiles, tiles_per_seq, final):
    (out_ref, na_ref, np_ref, nc_ref,
     zbuf, gbuf, ubuf, vbuf, poolbuf, mixbuf, nbuf, h2buf, mbuf, actbuf) = rest
    g_mix, g_mlp, g_ple, g_fin, cab, lng, lnb, pscale = _vector_params(vecs)
    s = pl.program_id(0)
    t = s % tiles_per_seq

    def shifted(buf, c, start):
        return buf[c, pl.ds(start, ROW_CHUNK, stride=1), :]

    def mixers(r0):
        rows = pl.ds(r0, ROW_CHUNK)
        for c in range(D_A // LANES):
            glu = zbuf[rows, _lane_cols(_O_AV, c)] * jax.nn.sigmoid(zbuf[rows, _lane_cols(_O_AG, c)])
            gbuf[c, pl.ds(r0 + HIST_A, ROW_CHUNK), :] = glu
        for c in range(D_B // LANES):
            ubuf[c, pl.ds(r0 + HIST_P, ROW_CHUNK), :] = zbuf[rows, _lane_cols(_O_UB, c)]
        for c in range(D_C // LANES):
            vbuf[c, pl.ds(r0 + HIST_C, ROW_CHUNK), :] = (
                zbuf[rows, _lane_cols(_O_CC, c)] * zbuf[rows, _lane_cols(_O_CX, c)])

        conv_a = []
        for c in range(D_A // LANES):
            acc = jnp.broadcast_to(cab[:, _lane_cols(0, c)], (ROW_CHUNK, LANES))
            for k in range(CONV_A_WIDTH):
                acc = acc + caw[k:k + 1, _lane_cols(0, c)] * shifted(gbuf, c, r0 + HIST_A - N_HIST_A + k)
            conv_a.append(acc)
        out_a = _layer_norm_silu(jnp.concatenate(conv_a, axis=-1), lng[...], lnb[...])
        mixbuf[rows, _M_A:_M_A + D_A] = out_a.astype(BF16)

        pos1 = (t * tile + r0 + 1) + lax.broadcasted_iota(jnp.int32, (ROW_CHUNK, LANES), 0)
        u_lo = [shifted(ubuf, 0, r0 + HIST_P - j) for j in range(4)]
        s2 = u_lo[0] + u_lo[1]
        s4 = s2 + (u_lo[2] + u_lo[3])
        pooled_lo = _pool_select(s2, s4, 2, 4, pos1) - u_lo[0]
        u_hi = [shifted(ubuf, 1, r0 + HIST_P - j) for j in range(16)]
        s8 = u_hi[0]
        for j in range(1, 8):
            s8 = s8 + u_hi[j]
        s16 = s8
        for j in range(8, 16):
            s16 = s16 + u_hi[j]
        pooled_hi = _pool_select(s8, s16, 8, 16, pos1) - u_hi[0]
        poolbuf[rows, 0:LANES] = pooled_lo.astype(BF16)
        poolbuf[rows, LANES:2 * LANES] = pooled_hi.astype(BF16)

        for c in range(D_C // LANES):
            conv_c = ccw[2:3, _lane_cols(0, c)] * shifted(vbuf, c, r0 + HIST_C)
            conv_c = conv_c + ccw[1:2, _lane_cols(0, c)] * shifted(vbuf, c, r0 + HIST_C - 1)
            conv_c = conv_c + ccw[0:1, _lane_cols(0, c)] * shifted(vbuf, c, r0 + HIST_C - 2)
            out_c = zbuf[rows, _lane_cols(_O_CB, c)] * conv_c
            mixbuf[rows, _lane_cols(_M_C, c)] = out_c.astype(BF16)

    n_ff = D_FF // FF_SLICE
    chunks_per_slice = tile // ROW_CHUNK // n_ff

    def mixer_blocks(j):
        for q in range(chunks_per_slice):
            mixers((j * chunks_per_slice + q) * ROW_CHUNK)

    def up_act(j):
        up = _dot(mbuf[...], w_up[:, j * FF_SLICE:(j + 1) * FF_SLICE])
        actbuf[j % 2] = jnp.square(jnp.maximum(up, 0.0)).astype(BF16)

    def step(front, back):
        if front:
            @pl.when(t == 0)
            def _():
                gbuf[:, 0:HIST_A, :] = jnp.zeros((D_A // LANES, HIST_A, LANES), F32)
                ubuf[:, 0:HIST_P, :] = jnp.zeros((D_B // LANES, HIST_P, LANES), F32)
                vbuf[:, 0:HIST_C, :] = jnp.zeros((D_C // LANES, HIST_C, LANES), F32)

        if front and not back:
            nbuf[...] = _rms_norm(hb_ref[...], g_mix[...]).astype(BF16)
        if back:
            h1 = hb_ref[...] + _dot(mixbuf[...], w_out[...])
        if front:
            zbuf[...] = _dot(nbuf[...], w_in[...])
            nbuf[...] = _rms_norm(hn_ref[...], g_mix[...]).astype(BF16)
        if back:
            h2buf[...] = h1
            mbuf[...] = _rms_norm(h1, g_mlp[...]).astype(BF16)
            up_act(0)

        for j in range(n_ff):
            if front:
                mixer_blocks(j)
            if back:
                down = _dot(actbuf[j % 2], w_down[j * FF_SLICE:(j + 1) * FF_SLICE, :])
                if j + 1 < n_ff:
                    up_act(j + 1)
                h2buf[...] += down

        if front:
            out_b = _dot(poolbuf[...], pool_bd[...]) * pscale[...]
            mixbuf[:, _M_B:_M_B + D_B] = out_b.astype(BF16)
            gbuf[:, 0:HIST_A, :] = gbuf[:, tile:tile + HIST_A, :]
            ubuf[:, 0:HIST_P, :] = ubuf[:, tile:tile + HIST_P, :]
            vbuf[:, 0:HIST_C, :] = vbuf[:, tile:tile + HIST_C, :]
        if back:
            out_ref[...] = _gated_embedding(h2buf[...], p_ref[...], g_ple, w_pg, w_pp, g_fin, final)

        if front:
            @pl.when(t == tiles_per_seq - 1)
            def _():
                for c in range(D_A // LANES):
                    na_ref[:, _lane_cols(0, c)] = gbuf[c, HIST_A + tile - N_HIST_A:HIST_A + tile, :]
                for c in range(D_B // LANES):
                    np_ref[:, _lane_cols(0, c)] = ubuf[c, HIST_P + tile - POOL_STATE:HIST_P + tile, :]
                for c in range(D_C // LANES):
                    nc_ref[:, _lane_cols(0, c)] = vbuf[c, HIST_C + tile - N_HIST_C:HIST_C + tile, :]

    pl.when(s == 0)(lambda: step(True, False))
    pl.when((s > 0) & (s < n_tiles))(lambda: step(True, True))
    pl.when(s == n_tiles)(lambda: step(False, True))


def _sample_kernel(h_ref, p_ref, sa_ref, sp_ref, sc_ref, vecs, w_in, caw, pool_bd, ccw, w_out, w_up, w_down, w_pg, w_pp,
                   out_ref, na_ref, np_ref, nc_ref, mixbuf, hbuf, h2buf, mbuf, *, pos0):
    g_mix, g_mlp, g_ple, g_fin, cab, lng, lnb, pscale = _vector_params(vecs)
    layer = pl.program_id(0)
    j = pl.program_id(1)

    @pl.when(j == 0)
    def _():
        @pl.when(layer == 0)
        def _():
            hbuf[...] = h_ref[...]

        h = hbuf[...]
        rows = h.shape[0]
        z = _dot(_rms_norm(h, g_mix[...]).astype(BF16), w_in[...])
        u = z[:, _O_UB:_O_UB + D_B]
        v = z[:, _O_CC:_O_CC + D_C] * z[:, _O_CX:_O_CX + D_C]
        glu = z[:, _O_AV:_O_AV + D_A] * jax.nn.sigmoid(z[:, _O_AG:_O_AG + D_A])

        for r0 in range(0, rows, SAMPLE_ROW_CHUNK):
            blk = slice(r0, r0 + SAMPLE_ROW_CHUNK)
            acc = cab[...] + caw[N_HIST_A:N_HIST_A + 1, :] * glu[blk]
            for k in range(N_HIST_A):
                acc = acc + caw[k:k + 1, :] * sa_ref[k, blk, :]
            mixbuf[blk, _M_A:_M_A + D_A] = _layer_norm_silu(acc, lng[...], lnb[...]).astype(BF16)
        na_ref[0:N_HIST_A - 1] = sa_ref[1:N_HIST_A]
        na_ref[N_HIST_A - 1] = glu

        pos1 = jnp.full((rows, LANES), pos0 + 1, jnp.int32)
        u_lo = u[:, 0:LANES]
        s2 = u_lo + sp_ref[POOL_STATE - 1, :, 0:LANES]
        s4 = s2 + (sp_ref[POOL_STATE - 2, :, 0:LANES] + sp_ref[POOL_STATE - 3, :, 0:LANES])
        pooled_lo = _pool_select(s2, s4, 2, 4, pos1) - u_lo
        u_hi = u[:, LANES:2 * LANES]
        s8 = u_hi
        for i in range(1, 8):
            s8 = s8 + sp_ref[POOL_STATE - i, :, LANES:2 * LANES]
        s16 = s8
        for i in range(8, 16):
            s16 = s16 + sp_ref[POOL_STATE - i, :, LANES:2 * LANES]
        pooled_hi = _pool_select(s8, s16, 8, 16, pos1) - u_hi
        pooled = jnp.concatenate([pooled_lo, pooled_hi], axis=-1).astype(BF16)
        mixbuf[:, _M_B:_M_B + D_B] = (_dot(pooled, pool_bd[...]) * pscale[...]).astype(BF16)
        np_ref[0:POOL_STATE - 1] = sp_ref[1:POOL_STATE]
        np_ref[POOL_STATE - 1] = u

        conv_c = ccw[2:3, :] * v + ccw[1:2, :] * sc_ref[1] + ccw[0:1, :] * sc_ref[0]
        mixbuf[:, _M_C:_M_C + D_C] = (z[:, _O_CB:_O_CB + D_C] * conv_c).astype(BF16)
        nc_ref[0] = sc_ref[1]
        nc_ref[1] = v

        h1 = h + _dot(mixbuf[...], w_out[...])
        h2buf[...] = h1
        mbuf[...] = _rms_norm(h1, g_mlp[...]).astype(BF16)

    act = jnp.square(jnp.maximum(_dot(mbuf[...], w_up[...]), 0.0)).astype(BF16)
    h2buf[...] += _dot(act, w_down[...])

    @pl.when(j == pl.num_programs(1) - 1)
    def _():
        h3 = _gated_embedding(h2buf[...], p_ref[...], g_ple, w_pg, w_pp, g_fin, False)
        hbuf[...] = h3

        @pl.when(layer == pl.num_programs(0) - 1)
        def _():
            out_ref[...] = _rms_norm(h3, g_fin[...])


def _resident():
    return pl.BlockSpec(memory_space=pltpu.VMEM)


def _layer_resident(arr, layer):
    zeros = (0,) * (arr.ndim - 1)
    return pl.BlockSpec((None,) + arr.shape[1:], lambda s: (layer,) + zeros, pipeline_mode=pl.Buffered(1))


def _prompt_layer(h, p_all, layer, weights, *, final):
    batch, seq, _ = h.shape
    tile = min(SEQ_TILE, seq)
    assert seq % tile == 0 and tile % (ROW_CHUNK * (D_FF // FF_SLICE)) == 0 and tile >= HIST_A
    tiles_per_seq = seq // tile
    n_tiles = batch * tiles_per_seq

    def front(s):
        i = jnp.minimum(s, n_tiles - 1)
        return i // tiles_per_seq, i % tiles_per_seq

    def back(s):
        i = jnp.maximum(s - 1, 0)
        return i // tiles_per_seq, i % tiles_per_seq

    def ahead(s):
        i = jnp.minimum(s + 1, n_tiles - 1)
        return i // tiles_per_seq, i % tiles_per_seq

    h_ahead = pl.BlockSpec((None, tile, D_MODEL), lambda s: (*ahead(s), 0))
    h_back = pl.BlockSpec((None, tile, D_MODEL), lambda s: (*back(s), 0))
    p_in = pl.BlockSpec((None, None, tile, D_PLE), lambda s: (layer, *back(s), 0))
    h_out = pl.BlockSpec((None, tile, D_MODEL), lambda s: (*back(s), 0))
    state_block = lambda rows, width: pl.BlockSpec((None, rows, width), lambda s: (front(s)[0], 0, 0))
    return pl.pallas_call(
        functools.partial(_prompt_kernel, tile=tile, n_tiles=n_tiles, tiles_per_seq=tiles_per_seq,
                          final=final),
        grid=(n_tiles + 1,),
        in_specs=[h_ahead, h_back, p_in] + [_layer_resident(w, layer) for w in weights],
        out_specs=[h_out, state_block(N_HIST_A, D_A),
                   state_block(POOL_STATE, D_B), state_block(N_HIST_C, D_C)],
        out_shape=[jax.ShapeDtypeStruct(h.shape, F32),
                   jax.ShapeDtypeStruct((batch, N_HIST_A, D_A), F32),
                   jax.ShapeDtypeStruct((batch, POOL_STATE, D_B), F32),
                   jax.ShapeDtypeStruct((batch, N_HIST_C, D_C), F32)],
        scratch_shapes=[pltpu.VMEM((tile, D_IN), F32),
                        pltpu.VMEM((D_A // LANES, HIST_A + tile, LANES), F32),
                        pltpu.VMEM((D_B // LANES, HIST_P + tile, LANES), F32),
                        pltpu.VMEM((D_C // LANES, HIST_C + tile, LANES), F32),
                        pltpu.VMEM((tile, D_B), BF16),
                        pltpu.VMEM((tile, D_MODEL), BF16),
                        pltpu.VMEM((tile, D_MODEL), BF16),
                        pltpu.VMEM((tile, D_MODEL), F32),
                        pltpu.VMEM((tile, D_MODEL), BF16),
                        pltpu.VMEM((2, tile, FF_SLICE), BF16)],
        compiler_params=pltpu.CompilerParams(
            dimension_semantics=("arbitrary",),
            vmem_limit_bytes=VMEM_LIMIT_BYTES),
        name="prompt_layer",
    )(h, h, p_all, *weights)


def _sample_layers(h, p_all, st_a, st_p, st_c, weights):
    depth = p_all.shape[0]
    rows = h.shape[0]
    n_ff = D_FF // FF_SLICE
    vecs, w_in, caw, pool_bd, ccw, w_out, w_up, w_down, w_pg, w_pp = weights

    def per_layer(arr, **kwargs):
        zeros = (0,) * (arr.ndim - 1)
        return pl.BlockSpec((None,) + arr.shape[1:], lambda l, j: (l,) + zeros, **kwargs)

    once = dict(pipeline_mode=pl.Buffered(1))
    return pl.pallas_call(
        functools.partial(_sample_kernel, pos0=PAST_LEN),
        grid=(depth, n_ff),
        in_specs=[_resident(), per_layer(p_all), per_layer(st_a, **once), per_layer(st_p, **once),
                  per_layer(st_c, **once), per_layer(vecs), per_layer(w_in), per_layer(caw), per_layer(pool_bd),
                  per_layer(ccw), per_layer(w_out),
                  pl.BlockSpec((None, D_MODEL, FF_SLICE), lambda l, j: (l, 0, j)),
                  pl.BlockSpec((None, FF_SLICE, D_MODEL), lambda l, j: (l, j, 0)),
                  per_layer(w_pg), per_layer(w_pp)],
        out_specs=[_resident(), per_layer(st_a), per_layer(st_p), per_layer(st_c)],
        out_shape=[jax.ShapeDtypeStruct(h.shape, F32),
                   jax.ShapeDtypeStruct(st_a.shape, F32),
                   jax.ShapeDtypeStruct(st_p.shape, F32),
                   jax.ShapeDtypeStruct(st_c.shape, F32)],
        scratch_shapes=[pltpu.VMEM((rows, D_MODEL), BF16), pltpu.VMEM((rows, D_MODEL), F32),
                        pltpu.VMEM((rows, D_MODEL), F32), pltpu.VMEM((rows, D_MODEL), BF16)],
        compiler_params=pltpu.CompilerParams(
            dimension_semantics=("arbitrary", "arbitrary"),
            vmem_limit_bytes=VMEM_LIMIT_BYTES),
        name="sample_layers",
    )(h, p_all, st_a, st_p, st_c, *weights)


def _pool_block_diag(pool_w):
    depth, groups, c, _ = pool_w.shape
    eye = jnp.eye(groups, dtype=pool_w.dtype)
    return (eye[None, :, None, :, None] * pool_w[:, :, :, None, :]).reshape(depth, groups * c, groups * c)


def _history_major(x):
    return jnp.transpose(x, (0, 2, 1, 3))


def kernel(x_prompt, x_sample, state_conv_a, state_pool, state_conv_c, p_prompt, p_sample, norm_mix_g, w_in, conv_a_w, conv_a_b, ln_a_g, ln_a_b, pool_w, pool_scale, conv_c_w, w_out, norm_mlp_g, w_up, w_down, norm_ple_g, w_ple_gate, w_ple_proj, final_norm_g):
    depth = w_in.shape[0]
    dec_batch, dec_seq, _ = x_sample.shape
    assert dec_seq == 1
    hp = x_prompt
    hs = x_sample.reshape(dec_batch, D_MODEL)
    ps = p_sample.reshape(depth, dec_batch, D_PLE)
    sa, sp, sc = (_history_major(x) for x in (state_conv_a, state_pool, state_conv_c))
    vectors = dict(g_mix=norm_mix_g, g_mlp=norm_mlp_g, g_ple=norm_ple_g,
                   g_fin=jnp.broadcast_to(final_norm_g, (depth, D_MODEL)),
                   cab=conv_a_b, lng=ln_a_g, lnb=ln_a_b, pscale=pool_scale)
    vecs = jnp.concatenate([vectors[name] for name, _ in _VEC_SIZES], axis=-1).reshape(depth, 1, _VEC_WIDTH)
    weights = (vecs, w_in.astype(BF16), conv_a_w, _pool_block_diag(pool_w).astype(BF16), conv_c_w,
               w_out.astype(BF16), w_up.astype(BF16), w_down.astype(BF16),
               w_ple_gate.astype(BF16), w_ple_proj.astype(BF16))
    prompt_states = []
    for i in range(depth):
        hp, *states = _prompt_layer(hp, p_prompt, i, weights, final=i == depth - 1)
        prompt_states.append(states)
    hs, *sample_states = _sample_layers(hs, ps, sa, sp, sc, weights)
    return (hp, hs.reshape(x_sample.shape), *(jnp.stack(x) for x in zip(*prompt_states)),
            *(_history_major(x) for x in sample_states))
```

```python
import functools

import jax
import jax.numpy as jnp
from jax import lax
from jax.experimental import pallas as pl
from jax.experimental.pallas import tpu as pltpu

D_MODEL = 1024
D_A = 384
D_B = 256
D_C = 384
D_IN = 2 * D_A + D_B + 3 * D_C
D_FF = 4096
D_PLE = 256
POOL_WINDOWS = (2, 4, 8, 16)
POOL_GROUP_DIM = D_B // len(POOL_WINDOWS)
POOL_STATE = 15
CONV_A_WIDTH = 31
CONV_C_WIDTH = 3
N_HIST_A = CONV_A_WIDTH - 1
N_HIST_C = CONV_C_WIDTH - 1
PAST_LEN = 16384
EPS = 1e-6

_O_AV = 0
_O_AG = D_A
_O_UB = 2 * D_A
_O_CB = _O_UB + D_B
_O_CC = _O_CB + D_C
_O_CX = _O_CC + D_C
_M_A = 0
_M_B = D_A
_M_C = D_A + D_B

LANES = 128
HIST_A = 32
HIST_P = 16
HIST_C = 8
ROW_CHUNK = 64
SAMPLE_ROW_CHUNK = 32
FF_SLICE = 1024
SEQ_TILE = 512
CAST_ROWS_IN = 128
CAST_ROWS_SQUARE = 512
CAST_FF_SLICE = 512
VMEM_LIMIT_BYTES = 62 * 1024 * 1024

BF16 = jnp.bfloat16
F32 = jnp.float32


_VEC_SIZES = (("g_mix", D_MODEL), ("g_mlp", D_MODEL), ("g_ple", D_MODEL), ("g_fin", D_MODEL),
              ("cab", D_A), ("lng", D_A), ("lnb", D_A), ("pscale", D_B))
_VEC_LAYOUT = {}
_VEC_WIDTH = 0
for _name, _size in _VEC_SIZES:
    _VEC_LAYOUT[_name] = (_VEC_WIDTH, _size)
    _VEC_WIDTH += _size


class _Cols:
    def __init__(self, ref, start, size):
        self.ref, self.start, self.size = ref, start, size

    def __getitem__(self, idx):
        if idx is Ellipsis:
            return self.ref[:, self.start:self.start + self.size]
        rows, cols = idx
        lo = cols.start or 0
        hi = self.size if cols.stop is None else cols.stop
        return self.ref[rows, self.start + lo:self.start + hi]


def _vector_params(vecs):
    return tuple(_Cols(vecs, *_VEC_LAYOUT[name]) for name, _ in _VEC_SIZES)


def _rms_norm(x, g):
    ms = jnp.mean(x * x, axis=-1, keepdims=True)
    return x * lax.rsqrt(ms + EPS) * g


def _dot(a, b):
    return jnp.dot(a, b, preferred_element_type=F32)


def _layer_norm_silu(ya, g, b):
    mu = jnp.mean(ya, axis=-1, keepdims=True)
    xc = ya - mu
    var = jnp.mean(xc * xc, axis=-1, keepdims=True)
    y = xc * lax.rsqrt(var + EPS) * g + b
    return y * jax.nn.sigmoid(y)


def _pool_select(s_small, s_big, w_small, w_big, pos1):
    lane = lax.broadcasted_iota(jnp.int32, s_small.shape, 1)
    cnt_small = jnp.minimum(pos1, w_small).astype(F32)
    cnt_big = jnp.minimum(pos1, w_big).astype(F32)
    return jnp.where(lane < POOL_GROUP_DIM, s_small / cnt_small, s_big / cnt_big)


def _lane_cols(offset, c):
    return slice(offset + c * LANES, offset + (c + 1) * LANES)


def _gated_embedding(h2, p, g_ple, w_pg, w_pp, g_fin, final):
    r3 = _rms_norm(h2, g_ple[...]).astype(BF16)
    gate = jax.nn.sigmoid(_dot(r3, w_pg[...]))
    h3 = h2 + gate * _dot(p.astype(BF16), w_pp[...])
    if final:
        h3 = _rms_norm(h3, g_fin[...])
    return h3


def _prompt_kernel(hn_ref, hb_ref, p_ref, vecs, w_in, caw, pool_bd, ccw, w_out, w_up, w_down, w_pg, w_pp, *rest,
                   tile, n_tiles, tiles_per_seq, final):
    (out_ref, na_ref, np_ref, nc_ref,
     zbuf, gbuf, ubuf, vbuf, poolbuf, mixbuf, nbuf, h2buf, mbuf, actbuf) = rest
    g_mix, g_mlp, g_ple, g_fin, cab, lng, lnb, pscale = _vector_params(vecs)
    s = pl.program_id(0)
    t = s % tiles_per_seq

    def shifted(buf, c, start):
        return buf[c, pl.ds(start, ROW_CHUNK, stride=1), :]

    def mixers(r0):
        rows = pl.ds(r0, ROW_CHUNK)
        for c in range(D_A // LANES):
            glu = zbuf[rows, _lane_cols(_O_AV, c)] * jax.nn.sigmoid(zbuf[rows, _lane_cols(_O_AG, c)])
            gbuf[c, pl.ds(r0 + HIST_A, ROW_CHUNK), :] = glu
        for c in range(D_B // LANES):
            ubuf[c, pl.ds(r0 + HIST_P, ROW_CHUNK), :] = zbuf[rows, _lane_cols(_O_UB, c)]
        for c in range(D_C // LANES):
            vbuf[c, pl.ds(r0 + HIST_C, ROW_CHUNK), :] = (
                zbuf[rows, _lane_cols(_O_CC, c)] * zbuf[rows, _lane_cols(_O_CX, c)])

        conv_a = []
        for c in range(D_A // LANES):
            acc = jnp.broadcast_to(cab[:, _lane_cols(0, c)], (ROW_CHUNK, LANES))
            for k in range(CONV_A_WIDTH):
                acc = acc + caw[k:k + 1, _lane_cols(0, c)] * shifted(gbuf, c, r0 + HIST_A - N_HIST_A + k)
            conv_a.append(acc)
        out_a = _layer_norm_silu(jnp.concatenate(conv_a, axis=-1), lng[...], lnb[...])
        mixbuf[rows, _M_A:_M_A + D_A] = out_a.astype(BF16)

        pos1 = (t * tile + r0 + 1) + lax.broadcasted_iota(jnp.int32, (ROW_CHUNK, LANES), 0)
        u_lo = [shifted(ubuf, 0, r0 + HIST_P - j) for j in range(4)]
        s2 = u_lo[0] + u_lo[1]
        s4 = s2 + (u_lo[2] + u_lo[3])
        pooled_lo = _pool_select(s2, s4, 2, 4, pos1) - u_lo[0]
        u_hi = [shifted(ubuf, 1, r0 + HIST_P - j) for j in range(16)]
        s8 = u_hi[0]
        for j in range(1, 8):
            s8 = s8 + u_hi[j]
        s16 = s8
        for j in range(8, 16):
            s16 = s16 + u_hi[j]
        pooled_hi = _pool_select(s8, s16, 8, 16, pos1) - u_hi[0]
        poolbuf[rows, 0:LANES] = pooled_lo.astype(BF16)
        poolbuf[rows, LANES:2 * LANES] = pooled_hi.astype(BF16)

        for c in range(D_C // LANES):
            conv_c = ccw[2:3, _lane_cols(0, c)] * shifted(vbuf, c, r0 + HIST_C)
            conv_c = conv_c + ccw[1:2, _lane_cols(0, c)] * shifted(vbuf, c, r0 + HIST_C - 1)
            conv_c = conv_c + ccw[0:1, _lane_cols(0, c)] * shifted(vbuf, c, r0 + HIST_C - 2)
            out_c = zbuf[rows, _lane_cols(_O_CB, c)] * conv_c
            mixbuf[rows, _lane_cols(_M_C, c)] = out_c.astype(BF16)

    n_ff = D_FF // FF_SLICE
    chunks_per_slice = tile // ROW_CHUNK // n_ff

    def mixer_blocks(j):
        for q in range(chunks_per_slice):
            mixers((j * chunks_per_slice + q) * ROW_CHUNK)

    def up_act(j):
        up = _dot(mbuf[...], w_up[:, j * FF_SLICE:(j + 1) * FF_SLICE])
        actbuf[j % 2] = jnp.square(jnp.maximum(up, 0.0)).astype(BF16)

    def step(front, back):
        if front:
            @pl.when(t == 0)
            def _():
                gbuf[:, 0:HIST_A, :] = jnp.zeros((D_A // LANES, HIST_A, LANES), F32)
                ubuf[:, 0:HIST_P, :] = jnp.zeros((D_B // LANES, HIST_P, LANES), F32)
                vbuf[:, 0:HIST_C, :] = jnp.zeros((D_C // LANES, HIST_C, LANES), F32)

        if front and not back:
            nbuf[...] = _rms_norm(hb_ref[...], g_mix[...]).astype(BF16)
        if back:
            h1 = hb_ref[...] + _dot(mixbuf[...], w_out[...])
        if front:
            zbuf[...] = _dot(nbuf[...], w_in[...])
            nbuf[...] = _rms_norm(hn_ref[...], g_mix[...]).astype(BF16)
        if back:
            h2buf[...] = h1
            mbuf[...] = _rms_norm(h1, g_mlp[...]).astype(BF16)
            up_act(0)

        for j in range(n_ff):
            if front:
                mixer_blocks(j)
            if back:
                down = _dot(actbuf[j % 2], w_down[j * FF_SLICE:(j + 1) * FF_SLICE, :])
                if j + 1 < n_ff:
                    up_act(j + 1)
                h2buf[...] += down

        if front:
            out_b = _dot(poolbuf[...], pool_bd[...]) * pscale[...]
            mixbuf[:, _M_B:_M_B + D_B] = out_b.astype(BF16)
            gbuf[:, 0:HIST_A, :] = gbuf[:, tile:tile + HIST_A, :]
            ubuf[:, 0:HIST_P, :] = ubuf[:, tile:tile + HIST_P, :]
            vbuf[:, 0:HIST_C, :] = vbuf[:, tile:tile + HIST_C, :]
        if back:
            out_ref[...] = _gated_embedding(h2buf[...], p_ref[...], g_ple, w_pg, w_pp, g_fin, final)

        if front:
            @pl.when(t == tiles_per_seq - 1)
            def _():
                for c in range(D_A // LANES):
                    na_ref[:, _lane_cols(0, c)] = gbuf[c, HIST_A + tile - N_HIST_A:HIST_A + tile, :]
                for c in range(D_B // LANES):
                    np_ref[:, _lane_cols(0, c)] = ubuf[c, HIST_P + tile - POOL_STATE:HIST_P + tile, :]
                for c in range(D_C // LANES):
                    nc_ref[:, _lane_cols(0, c)] = vbuf[c, HIST_C + tile - N_HIST_C:HIST_C + tile, :]

    pl.when(s == 0)(lambda: step(True, False))
    pl.when((s > 0) & (s < n_tiles))(lambda: step(True, True))
    pl.when(s == n_tiles)(lambda: step(False, True))


def _sample_kernel(h_ref, p_ref, sa_ref, sp_ref, sc_ref, vecs, caw, pool_bd, ccw,
                   w_in, w_out, w_up, w_down, w_pg, w_pp,
                   out_ref, na_ref, np_ref, nc_ref, b_in, b_out, b_up, b_down, b_pg, b_pp,
                   hbuf, nbuf, zbuf, mixbuf, h2buf, mbuf, r3buf, gbuf, *, pos0):
    g_mix, g_mlp, g_ple, g_fin, cab, lng, lnb, pscale = _vector_params(vecs)
    layer = pl.program_id(0)
    j = pl.program_id(1)
    n_in, n_out, n_ff, n_pg = _sample_phase_steps()
    rows = hbuf.shape[0]

    def cast_emit(src, dst):
        w = src[...].astype(BF16)
        dst[...] = w
        return w

    def accumulate(acc_ref, part, first):
        if first:
            acc_ref[...] = part
        else:
            acc_ref[...] += part

    def mixers():
        z = zbuf[...]
        u = z[:, _O_UB:_O_UB + D_B]
        v = z[:, _O_CC:_O_CC + D_C] * z[:, _O_CX:_O_CX + D_C]
        glu = z[:, _O_AV:_O_AV + D_A] * jax.nn.sigmoid(z[:, _O_AG:_O_AG + D_A])

        for r0 in range(0, rows, SAMPLE_ROW_CHUNK):
            blk = slice(r0, r0 + SAMPLE_ROW_CHUNK)
            acc = cab[...] + caw[N_HIST_A:N_HIST_A + 1, :] * glu[blk]
            for k in range(N_HIST_A):
                acc = acc + caw[k:k + 1, :] * sa_ref[k, blk, :]
            mixbuf[blk, _M_A:_M_A + D_A] = _layer_norm_silu(acc, lng[...], lnb[...]).astype(BF16)
        na_ref[0:N_HIST_A - 1] = sa_ref[1:N_HIST_A]
        na_ref[N_HIST_A - 1] = glu

        pos1 = jnp.full((rows, LANES), pos0 + 1, jnp.int32)
        u_lo = u[:, 0:LANES]
        s2 = u_lo + sp_ref[POOL_STATE - 1, :, 0:LANES]
        s4 = s2 + (sp_ref[POOL_STATE - 2, :, 0:LANES] + sp_ref[POOL_STATE - 3, :, 0:LANES])
        pooled_lo = _pool_select(s2, s4, 2, 4, pos1) - u_lo
        u_hi = u[:, LANES:2 * LANES]
        s8 = u_hi
        for i in range(1, 8):
            s8 = s8 + sp_ref[POOL_STATE - i, :, LANES:2 * LANES]
        s16 = s8
        for i in range(8, 16):
            s16 = s16 + sp_ref[POOL_STATE - i, :, LANES:2 * LANES]
        pooled_hi = _pool_select(s8, s16, 8, 16, pos1) - u_hi
        pooled = jnp.concatenate([pooled_lo, pooled_hi], axis=-1).astype(BF16)
        mixbuf[:, _M_B:_M_B + D_B] = (_dot(pooled, pool_bd[...]) * pscale[...]).astype(BF16)
        np_ref[0:POOL_STATE - 1] = sp_ref[1:POOL_STATE]
        np_ref[POOL_STATE - 1] = u

        conv_c = ccw[2:3, :] * v + ccw[1:2, :] * sc_ref[1] + ccw[0:1, :] * sc_ref[0]
        mixbuf[:, _M_C:_M_C + D_C] = (z[:, _O_CB:_O_CB + D_C] * conv_c).astype(BF16)
        nc_ref[0] = sc_ref[1]
        nc_ref[1] = v

    def in_step(k):
        if k == 0:
            @pl.when(layer == 0)
            def _():
                hbuf[...] = h_ref[...]
            nbuf[...] = _rms_norm(hbuf[...], g_mix[...]).astype(BF16)
        w = cast_emit(w_in, b_in)
        cols = slice(k * CAST_ROWS_IN, (k + 1) * CAST_ROWS_IN)
        accumulate(zbuf, _dot(nbuf[:, cols], w), k == 0)
        if k == n_in - 1:
            mixers()

    def out_step(k):
        w = cast_emit(w_out, b_out)
        cols = slice(k * CAST_ROWS_SQUARE, (k + 1) * CAST_ROWS_SQUARE)
        part = _dot(mixbuf[:, cols], w)
        accumulate(h2buf, hbuf[...] + part if k == 0 else part, k == 0)
        if k == n_out - 1:
            mbuf[...] = _rms_norm(h2buf[...], g_mlp[...]).astype(BF16)

    def ff_step(k):
        del k
        up = _dot(mbuf[...], cast_emit(w_up, b_up))
        act = jnp.square(jnp.maximum(up, 0.0)).astype(BF16)
        h2buf[...] += _dot(act, cast_emit(w_down, b_down))

    def pg_step(k):
        if k == 0:
            r3buf[...] = _rms_norm(h2buf[...], g_ple[...]).astype(BF16)
        w = cast_emit(w_pg, b_pg)
        cols = slice(k * CAST_ROWS_SQUARE, (k + 1) * CAST_ROWS_SQUARE)
        accumulate(gbuf, _dot(r3buf[:, cols], w), k == 0)
        if k == n_pg - 1:
            gate = jax.nn.sigmoid(gbuf[...])
            h3 = h2buf[...] + gate * _dot(p_ref[...].astype(BF16), cast_emit(w_pp, b_pp))
            hbuf[...] = h3

            @pl.when(layer == pl.num_programs(0) - 1)
            def _():
                out_ref[...] = _rms_norm(h3, g_fin[...])

    first = 0
    for n_steps, phase in ((n_in, in_step), (n_out, out_step), (n_ff, ff_step), (n_pg, pg_step)):
        for k in range(n_steps):
            pl.when(j == first + k)(functools.partial(phase, k))
        first += n_steps


def _sample_phase_steps():
    return (D_MODEL // CAST_ROWS_IN, D_MODEL // CAST_ROWS_SQUARE, D_FF // CAST_FF_SLICE, D_MODEL // CAST_ROWS_SQUARE)


def _resident():
    return pl.BlockSpec(memory_space=pltpu.VMEM)


def _layer_resident(arr, layer):
    zeros = (0,) * (arr.ndim - 1)
    return pl.BlockSpec((None,) + arr.shape[1:], lambda s: (layer,) + zeros, pipeline_mode=pl.Buffered(1))


def _prompt_layer(h, p_all, layer, weights, *, final):
    batch, seq, _ = h.shape
    tile = min(SEQ_TILE, seq)
    assert seq % tile == 0 and tile % (ROW_CHUNK * (D_FF // FF_SLICE)) == 0 and tile >= HIST_A
    tiles_per_seq = seq // tile
    n_tiles = batch * tiles_per_seq

    def front(s):
        i = jnp.minimum(s, n_tiles - 1)
        return i // tiles_per_seq, i % tiles_per_seq

    def back(s):
        i = jnp.maximum(s - 1, 0)
        return i // tiles_per_seq, i % tiles_per_seq

    def ahead(s):
        i = jnp.minimum(s + 1, n_tiles - 1)
        return i // tiles_per_seq, i % tiles_per_seq

    h_ahead = pl.BlockSpec((None, tile, D_MODEL), lambda s: (*ahead(s), 0))
    h_back = pl.BlockSpec((None, tile, D_MODEL), lambda s: (*back(s), 0))
    p_in = pl.BlockSpec((None, None, tile, D_PLE), lambda s: (layer, *back(s), 0))
    h_out = pl.BlockSpec((None, tile, D_MODEL), lambda s: (*back(s), 0))
    state_block = lambda rows, width: pl.BlockSpec((None, rows, width), lambda s: (front(s)[0], 0, 0))
    return pl.pallas_call(
        functools.partial(_prompt_kernel, tile=tile, n_tiles=n_tiles, tiles_per_seq=tiles_per_seq,
                          final=final),
        grid=(n_tiles + 1,),
        in_specs=[h_ahead, h_back, p_in] + [_layer_resident(w, layer) for w in weights],
        out_specs=[h_out, state_block(N_HIST_A, D_A),
                   state_block(POOL_STATE, D_B), state_block(N_HIST_C, D_C)],
        out_shape=[jax.ShapeDtypeStruct(h.shape, F32),
                   jax.ShapeDtypeStruct((batch, N_HIST_A, D_A), F32),
                   jax.ShapeDtypeStruct((batch, POOL_STATE, D_B), F32),
                   jax.ShapeDtypeStruct((batch, N_HIST_C, D_C), F32)],
        scratch_shapes=[pltpu.VMEM((tile, D_IN), F32),
                        pltpu.VMEM((D_A // LANES, HIST_A + tile, LANES), F32),
                        pltpu.VMEM((D_B // LANES, HIST_P + tile, LANES), F32),
                        pltpu.VMEM((D_C // LANES, HIST_C + tile, LANES), F32),
                        pltpu.VMEM((tile, D_B), BF16),
                        pltpu.VMEM((tile, D_MODEL), BF16),
                        pltpu.VMEM((tile, D_MODEL), BF16),
                        pltpu.VMEM((tile, D_MODEL), F32),
                        pltpu.VMEM((tile, D_MODEL), BF16),
                        pltpu.VMEM((2, tile, FF_SLICE), BF16)],
        compiler_params=pltpu.CompilerParams(
            dimension_semantics=("arbitrary",),
            vmem_limit_bytes=VMEM_LIMIT_BYTES),
        name="prompt_layer",
    )(h, h, p_all, *weights)


def _sample_layers(h, p_all, st_a, st_p, st_c, small, big):
    depth = p_all.shape[0]
    rows = h.shape[0]
    n_in, n_out, n_ff, n_pg = _sample_phase_steps()
    o_out, o_ff, o_pg = n_in, n_in + n_out, n_in + n_out + n_ff

    def per_layer(arr, **kwargs):
        zeros = (0,) * (arr.ndim - 1)
        return pl.BlockSpec((None,) + arr.shape[1:], lambda l, j: (l,) + zeros, **kwargs)

    def phase(j, first, n):
        return jnp.clip(j - first, 0, n - 1)

    weight_specs = [
        pl.BlockSpec((None, CAST_ROWS_IN, D_IN), lambda l, j: (l, phase(j, 0, n_in), 0)),
        pl.BlockSpec((None, CAST_ROWS_SQUARE, D_MODEL), lambda l, j: (l, phase(j, o_out, n_out), 0)),
        pl.BlockSpec((None, D_MODEL, CAST_FF_SLICE), lambda l, j: (l, 0, phase(j, o_ff, n_ff))),
        pl.BlockSpec((None, CAST_FF_SLICE, D_MODEL), lambda l, j: (l, phase(j, o_ff, n_ff), 0)),
        pl.BlockSpec((None, CAST_ROWS_SQUARE, D_MODEL), lambda l, j: (l, phase(j, o_pg, n_pg), 0)),
        pl.BlockSpec((None, D_PLE, D_MODEL), lambda l, j: (l, 0, 0)),
    ]
    once = dict(pipeline_mode=pl.Buffered(1))
    outs = pl.pallas_call(
        functools.partial(_sample_kernel, pos0=PAST_LEN),
        grid=(depth, n_in + n_out + n_ff + n_pg),
        in_specs=[_resident(), per_layer(p_all), per_layer(st_a, **once), per_layer(st_p, **once),
                  per_layer(st_c, **once)] + [per_layer(x) for x in small] + weight_specs,
        out_specs=[_resident(), per_layer(st_a), per_layer(st_p), per_layer(st_c)] + weight_specs,
        out_shape=[jax.ShapeDtypeStruct(h.shape, F32),
                   jax.ShapeDtypeStruct(st_a.shape, F32),
                   jax.ShapeDtypeStruct(st_p.shape, F32),
                   jax.ShapeDtypeStruct(st_c.shape, F32)]
                  + [jax.ShapeDtypeStruct(w.shape, BF16) for w in big],
        scratch_shapes=[pltpu.VMEM((rows, D_MODEL), F32),
                        pltpu.VMEM((rows, D_MODEL), BF16),
                        pltpu.VMEM((rows, D_IN), F32),
                        pltpu.VMEM((rows, D_MODEL), BF16),
                        pltpu.VMEM((rows, D_MODEL), F32),
                        pltpu.VMEM((rows, D_MODEL), BF16),
                        pltpu.VMEM((rows, D_MODEL), BF16),
                        pltpu.VMEM((rows, D_MODEL), F32)],
        compiler_params=pltpu.CompilerParams(
            dimension_semantics=("arbitrary", "arbitrary"),
            vmem_limit_bytes=VMEM_LIMIT_BYTES),
        name="sample_layers",
    )(h, p_all, st_a, st_p, st_c, *small, *big)
    return outs[:4], outs[4:]


def _pool_block_diag(pool_w):
    depth, groups, c, _ = pool_w.shape
    eye = jnp.eye(groups, dtype=pool_w.dtype)
    return (eye[None, :, None, :, None] * pool_w[:, :, :, None, :]).reshape(depth, groups * c, groups * c)


def _history_major(x):
    return jnp.transpose(x, (0, 2, 1, 3))


def kernel(x_prompt, x_sample, state_conv_a, state_pool, state_conv_c, p_prompt, p_sample, norm_mix_g, w_in, conv_a_w, conv_a_b, ln_a_g, ln_a_b, pool_w, pool_scale, conv_c_w, w_out, norm_mlp_g, w_up, w_down, norm_ple_g, w_ple_gate, w_ple_proj, final_norm_g):
    depth = w_in.shape[0]
    dec_batch, dec_seq, _ = x_sample.shape
    assert dec_seq == 1
    hp = x_prompt
    hs = x_sample.reshape(dec_batch, D_MODEL)
    ps = p_sample.reshape(depth, dec_batch, D_PLE)
    sa, sp, sc = (_history_major(x) for x in (state_conv_a, state_pool, state_conv_c))
    vectors = dict(g_mix=norm_mix_g, g_mlp=norm_mlp_g, g_ple=norm_ple_g,
                   g_fin=jnp.broadcast_to(final_norm_g, (depth, D_MODEL)),
                   cab=conv_a_b, lng=ln_a_g, lnb=ln_a_b, pscale=pool_scale)
    vecs = jnp.concatenate([vectors[name] for name, _ in _VEC_SIZES], axis=-1).reshape(depth, 1, _VEC_WIDTH)
    pool_bd = _pool_block_diag(pool_w).astype(BF16)
    (hs, *sample_states), (b_in, b_out, b_up, b_down, b_pg, b_pp) = _sample_layers(
        hs, ps, sa, sp, sc, (vecs, conv_a_w, pool_bd, conv_c_w),
        (w_in, w_out, w_up, w_down, w_ple_gate, w_ple_proj))
    weights = (vecs, b_in, conv_a_w, pool_bd, conv_c_w, b_out, b_up, b_down, b_pg, b_pp)
    prompt_states = []
    for i in range(depth):
        hp, *states = _prompt_layer(hp, p_prompt, i, weights, final=i == depth - 1)
        prompt_states.append(states)
    return (hp, hs.reshape(x_sample.shape), *(jnp.stack(x) for x in zip(*prompt_states)),
            *(_history_major(x) for x in sample_states))
```

```python
import functools

import jax
import jax.numpy as jnp
from jax import lax
from jax.experimental import pallas as pl
from jax.experimental.pallas import tpu as pltpu

D_MODEL = 1024
D_A = 384
D_B = 256
D_C = 384
D_IN = 2 * D_A + D_B + 3 * D_C
D_FF = 4096
D_PLE = 256
POOL_WINDOWS = (2, 4, 8, 16)
POOL_GROUP_DIM = D_B // len(POOL_WINDOWS)
POOL_STATE = 15
CONV_A_WIDTH = 31
CONV_C_WIDTH = 3
N_HIST_A = CONV_A_WIDTH - 1
N_HIST_C = CONV_C_WIDTH - 1
PAST_LEN = 16384
EPS = 1e-6

_O_AV = 0
_O_AG = D_A
_O_UB = 2 * D_A
_O_CB = _O_UB + D_B
_O_CC = _O_CB + D_C
_O_CX = _O_CC + D_C
_M_A = 0
_M_B = D_A
_M_C = D_A + D_B

LANES = 128
HIST_A = 32
HIST_P = 16
HIST_C = 8
ROW_CHUNK = 64
SAMPLE_ROW_CHUNK = 32
FF_SLICE = 1024
SEQ_TILE = 512
VMEM_LIMIT_BYTES = 62 * 1024 * 1024

BF16 = jnp.bfloat16
F32 = jnp.float32


_VEC_SIZES = (("g_mix", D_MODEL), ("g_mlp", D_MODEL), ("g_ple", D_MODEL), ("g_fin", D_MODEL),
              ("cab", D_A), ("lng", D_A), ("lnb", D_A), ("pscale", D_B))
_VEC_LAYOUT = {}
_VEC_WIDTH = 0
for _name, _size in _VEC_SIZES:
    _VEC_LAYOUT[_name] = (_VEC_WIDTH, _size)
    _VEC_WIDTH += _size


class _Cols:
    def __init__(self, ref, start, size):
        self.ref, self.start, self.size = ref, start, size

    def __getitem__(self, idx):
        if idx is Ellipsis:
            return self.ref[:, self.start:self.start + self.size]
        rows, cols = idx
        lo = cols.start or 0
        hi = self.size if cols.stop is None else cols.stop
        return self.ref[rows, self.start + lo:self.start + hi]


def _vector_params(vecs):
    return tuple(_Cols(vecs, *_VEC_LAYOUT[name]) for name, _ in _VEC_SIZES)


def _rms_norm(x, g):
    ms = jnp.mean(x * x, axis=-1, keepdims=True)
    return x * lax.rsqrt(ms + EPS) * g


def _dot(a, b):
    return jnp.dot(a, b, preferred_element_type=F32)


def _layer_norm_silu(ya, g, b):
    mu = jnp.mean(ya, axis=-1, keepdims=True)
    xc = ya - mu
    var = jnp.mean(xc * xc, axis=-1, keepdims=True)
    y = xc * lax.rsqrt(var + EPS) * g + b
    return y * jax.nn.sigmoid(y)


def _pool_select(s_small, s_big, w_small, w_big, pos1):
    lane = lax.broadcasted_iota(jnp.int32, s_small.shape, 1)
    cnt_small = jnp.minimum(pos1, w_small).astype(F32)
    cnt_big = jnp.minimum(pos1, w_big).astype(F32)
    return jnp.where(lane < POOL_GROUP_DIM, s_small / cnt_small, s_big / cnt_big)


def _lane_cols(offset, c):
    return slice(offset + c * LANES, offset + (c + 1) * LANES)


def _gated_embedding(h2, p, g_ple, w_pg, w_pp, g_fin, final):
    r3 = _rms_norm(h2, g_ple[...]).astype(BF16)
    gate = jax.nn.sigmoid(_dot(r3, w_pg[...]))
    h3 = h2 + gate * _dot(p.astype(BF16), w_pp[...])
    if final:
        h3 = _rms_norm(h3, g_fin[...])
    return h3


def _prompt_kernel(hn_ref, hb_ref, p_ref, vecs, w_in, caw, pool_bd, ccw, w_out, w_up, w_down, w_pg, w_pp, *rest,
                   tile, n_tiles, tiles_per_seq, final):
    (out_ref, na_ref, np_ref, nc_ref,
     zbuf, gbuf, ubuf, vbuf, poolbuf, mixbuf, nbuf, h2buf, mbuf, actbuf) = rest
    g_mix, g_mlp, g_ple, g_fin, cab, lng, lnb, pscale = _vector_params(vecs)
    s = pl.program_id(0)
    t = s % tiles_per_seq

    def shifted(buf, c, start):
        return buf[c, pl.ds(start, ROW_CHUNK, stride=1), :]

    def mixers(r0):
        rows = pl.ds(r0, ROW_CHUNK)
        for c in range(D_A // LANES):
            glu = zbuf[rows, _lane_cols(_O_AV, c)] * jax.nn.sigmoid(zbuf[rows, _lane_cols(_O_AG, c)])
            gbuf[c, pl.ds(r0 + HIST_A, ROW_CHUNK), :] = glu
        for c in range(D_B // LANES):
            ubuf[c, pl.ds(r0 + HIST_P, ROW_CHUNK), :] = zbuf[rows, _lane_cols(_O_UB, c)]
        for c in range(D_C // LANES):
            vbuf[c, pl.ds(r0 + HIST_C, ROW_CHUNK), :] = (
                zbuf[rows, _lane_cols(_O_CC, c)] * zbuf[rows, _lane_cols(_O_CX, c)])

        conv_a = []
        for c in range(D_A // LANES):
            acc = jnp.broadcast_to(cab[:, _lane_cols(0, c)], (ROW_CHUNK, LANES))
            for k in range(CONV_A_WIDTH):
                acc = acc + caw[k:k + 1, _lane_cols(0, c)] * shifted(gbuf, c, r0 + HIST_A - N_HIST_A + k)
            conv_a.append(acc)
        out_a = _layer_norm_silu(jnp.concatenate(conv_a, axis=-1), lng[...], lnb[...])
        mixbuf[rows, _M_A:_M_A + D_A] = out_a.astype(BF16)

        pos1 = (t * tile + r0 + 1) + lax.broadcasted_iota(jnp.int32, (ROW_CHUNK, LANES), 0)
        u_lo = [shifted(ubuf, 0, r0 + HIST_P - j) for j in range(4)]
        s2 = u_lo[0] + u_lo[1]
        s4 = s2 + (u_lo[2] + u_lo[3])
        pooled_lo = _pool_select(s2, s4, 2, 4, pos1) - u_lo[0]
        u_hi = [shifted(ubuf, 1, r0 + HIST_P - j) for j in range(16)]
        s8 = u_hi[0]
        for j in range(1, 8):
            s8 = s8 + u_hi[j]
        s16 = s8
        for j in range(8, 16):
            s16 = s16 + u_hi[j]
        pooled_hi = _pool_select(s8, s16, 8, 16, pos1) - u_hi[0]
        poolbuf[rows, 0:LANES] = pooled_lo.astype(BF16)
        poolbuf[rows, LANES:2 * LANES] = pooled_hi.astype(BF16)

        for c in range(D_C // LANES):
            conv_c = ccw[2:3, _lane_cols(0, c)] * shifted(vbuf, c, r0 + HIST_C)
            conv_c = conv_c + ccw[1:2, _lane_cols(0, c)] * shifted(vbuf, c, r0 + HIST_C - 1)
            conv_c = conv_c + ccw[0:1, _lane_cols(0, c)] * shifted(vbuf, c, r0 + HIST_C - 2)
            out_c = zbuf[rows, _lane_cols(_O_CB, c)] * conv_c
            mixbuf[rows, _lane_cols(_M_C, c)] = out_c.astype(BF16)

    n_ff = D_FF // FF_SLICE
    chunks_per_slice = tile // ROW_CHUNK // n_ff

    def mixer_blocks(j):
        for q in range(chunks_per_slice):
            mixers((j * chunks_per_slice + q) * ROW_CHUNK)

    def up_act(j):
        up = _dot(mbuf[...], w_up[:, j * FF_SLICE:(j + 1) * FF_SLICE])
        actbuf[j % 2] = jnp.square(jnp.maximum(up, 0.0)).astype(BF16)

    def step(front, back):
        if front:
            @pl.when(t == 0)
            def _():
                gbuf[:, 0:HIST_A, :] = jnp.zeros((D_A // LANES, HIST_A, LANES), F32)
                ubuf[:, 0:HIST_P, :] = jnp.zeros((D_B // LANES, HIST_P, LANES), F32)
                vbuf[:, 0:HIST_C, :] = jnp.zeros((D_C // LANES, HIST_C, LANES), F32)

        if front and not back:
            nbuf[...] = _rms_norm(hb_ref[...], g_mix[...]).astype(BF16)
        if back:
            h1 = hb_ref[...] + _dot(mixbuf[...], w_out[...])
        if front:
            zbuf[...] = _dot(nbuf[...], w_in[...])
            nbuf[...] = _rms_norm(hn_ref[...], g_mix[...]).astype(BF16)
        if back:
            h2buf[...] = h1
            mbuf[...] = _rms_norm(h1, g_mlp[...]).astype(BF16)
            up_act(0)

        for j in range(n_ff):
            if front:
                mixer_blocks(j)
            if back:
                down = _dot(actbuf[j % 2], w_down[j * FF_SLICE:(j + 1) * FF_SLICE, :])
                if j + 1 < n_ff:
                    up_act(j + 1)
                h2buf[...] += down

        if front:
            out_b = _dot(poolbuf[...], pool_bd[...]) * pscale[...]
            mixbuf[:, _M_B:_M_B + D_B] = out_b.astype(BF16)
            gbuf[:, 0:HIST_A, :] = gbuf[:, tile:tile + HIST_A, :]
            ubuf[:, 0:HIST_P, :] = ubuf[:, tile:tile + HIST_P, :]
            vbuf[:, 0:HIST_C, :] = vbuf[:, tile:tile + HIST_C, :]
        if back:
            out_ref[...] = _gated_embedding(h2buf[...], p_ref[...], g_ple, w_pg, w_pp, g_fin, final)

        if front:
            @pl.when(t == tiles_per_seq - 1)
            def _():
                for c in range(D_A // LANES):
                    na_ref[:, _lane_cols(0, c)] = gbuf[c, HIST_A + tile - N_HIST_A:HIST_A + tile, :]
                for c in range(D_B // LANES):
                    np_ref[:, _lane_cols(0, c)] = ubuf[c, HIST_P + tile - POOL_STATE:HIST_P + tile, :]
                for c in range(D_C // LANES):
                    nc_ref[:, _lane_cols(0, c)] = vbuf[c, HIST_C + tile - N_HIST_C:HIST_C + tile, :]

    pl.when(s == 0)(lambda: step(True, False))
    pl.when((s > 0) & (s < n_tiles))(lambda: step(True, True))
    pl.when(s == n_tiles)(lambda: step(False, True))


def _sample_kernel(h_ref, p_ref, sa_ref, sp_ref, sc_ref, vecs, w_in, caw, pool_bd, ccw, w_out, w_up, w_down, w_pg, w_pp,
                   out_ref, na_ref, np_ref, nc_ref, mixbuf, hbuf, h2buf, mbuf, *, pos0):
    g_mix, g_mlp, g_ple, g_fin, cab, lng, lnb, pscale = _vector_params(vecs)
    layer = pl.program_id(0)
    j = pl.program_id(1)

    @pl.when(j == 0)
    def _():
        @pl.when(layer == 0)
        def _():
            hbuf[...] = h_ref[...]

        h = hbuf[...]
        rows = h.shape[0]
        z = _dot(_rms_norm(h, g_mix[...]).astype(BF16), w_in[...])
        u = z[:, _O_UB:_O_UB + D_B]
        v = z[:, _O_CC:_O_CC + D_C] * z[:, _O_CX:_O_CX + D_C]
        glu = z[:, _O_AV:_O_AV + D_A] * jax.nn.sigmoid(z[:, _O_AG:_O_AG + D_A])

        for r0 in range(0, rows, SAMPLE_ROW_CHUNK):
            blk = slice(r0, r0 + SAMPLE_ROW_CHUNK)
            acc = cab[...] + caw[N_HIST_A:N_HIST_A + 1, :] * glu[blk]
            for k in range(N_HIST_A):
                acc = acc + caw[k:k + 1, :] * sa_ref[k, blk, :]
            mixbuf[blk, _M_A:_M_A + D_A] = _layer_norm_silu(acc, lng[...], lnb[...]).astype(BF16)
        na_ref[0:N_HIST_A - 1] = sa_ref[1:N_HIST_A]
        na_ref[N_HIST_A - 1] = glu

        pos1 = jnp.full((rows, LANES), pos0 + 1, jnp.int32)
        u_lo = u[:, 0:LANES]
        s2 = u_lo + sp_ref[POOL_STATE - 1, :, 0:LANES]
        s4 = s2 + (sp_ref[POOL_STATE - 2, :, 0:LANES] + sp_ref[POOL_STATE - 3, :, 0:LANES])
        pooled_lo = _pool_select(s2, s4, 2, 4, pos1) - u_lo
        u_hi = u[:, LANES:2 * LANES]
        s8 = u_hi
        for i in range(1, 8):
            s8 = s8 + sp_ref[POOL_STATE - i, :, LANES:2 * LANES]
        s16 = s8
        for i in range(8, 16):
            s16 = s16 + sp_ref[POOL_STATE - i, :, LANES:2 * LANES]
        pooled_hi = _pool_select(s8, s16, 8, 16, pos1) - u_hi
        pooled = jnp.concatenate([pooled_lo, pooled_hi], axis=-1).astype(BF16)
        mixbuf[:, _M_B:_M_B + D_B] = (_dot(pooled, pool_bd[...]) * pscale[...]).astype(BF16)
        np_ref[0:POOL_STATE - 1] = sp_ref[1:POOL_STATE]
        np_ref[POOL_STATE - 1] = u

        conv_c = ccw[2:3, :] * v + ccw[1:2, :] * sc_ref[1] + ccw[0:1, :] * sc_ref[0]
        mixbuf[:, _M_C:_M_C + D_C] = (z[:, _O_CB:_O_CB + D_C] * conv_c).astype(BF16)
        nc_ref[0] = sc_ref[1]
        nc_ref[1] = v

        h1 = h + _dot(mixbuf[...], w_out[...])
        h2buf[...] = h1
        mbuf[...] = _rms_norm(h1, g_mlp[...]).astype(BF16)

    act = jnp.square(jnp.maximum(_dot(mbuf[...], w_up[...]), 0.0)).astype(BF16)
    h2buf[...] += _dot(act, w_down[...])

    @pl.when(j == pl.num_programs(1) - 1)
    def _():
        h3 = _gated_embedding(h2buf[...], p_ref[...], g_ple, w_pg, w_pp, g_fin, False)
        hbuf[...] = h3

        @pl.when(layer == pl.num_programs(0) - 1)
        def _():
            out_ref[...] = _rms_norm(h3, g_fin[...])


def _resident():
    return pl.BlockSpec(memory_space=pltpu.VMEM)


def _layer_resident(arr, layer):
    zeros = (0,) * (arr.ndim - 1)
    return pl.BlockSpec((None,) + arr.shape[1:], lambda s: (layer,) + zeros, pipeline_mode=pl.Buffered(1))


def _prompt_layer(h, p_all, layer, weights, *, final):
    batch, seq, _ = h.shape
    tile = min(SEQ_TILE, seq)
    assert seq % tile == 0 and tile % (ROW_CHUNK * (D_FF // FF_SLICE)) == 0 and tile >= HIST_A
    tiles_per_seq = seq // tile
    n_tiles = batch * tiles_per_seq

    def front(s):
        i = jnp.minimum(s, n_tiles - 1)
        return i // tiles_per_seq, i % tiles_per_seq

    def back(s):
        i = jnp.maximum(s - 1, 0)
        return i // tiles_per_seq, i % tiles_per_seq

    def ahead(s):
        i = jnp.minimum(s + 1, n_tiles - 1)
        return i // tiles_per_seq, i % tiles_per_seq

    h_ahead = pl.BlockSpec((None, tile, D_MODEL), lambda s: (*ahead(s), 0))
    h_back = pl.BlockSpec((None, tile, D_MODEL), lambda s: (*back(s), 0))
    p_in = pl.BlockSpec((None, None, tile, D_PLE), lambda s: (layer, *back(s), 0))
    h_out = pl.BlockSpec((None, tile, D_MODEL), lambda s: (*back(s), 0))
    state_block = lambda rows, width: pl.BlockSpec((None, rows, width), lambda s: (front(s)[0], 0, 0))
    return pl.pallas_call(
        functools.partial(_prompt_kernel, tile=tile, n_tiles=n_tiles, tiles_per_seq=tiles_per_seq,
                          final=final),
        grid=(n_tiles + 1,),
        in_specs=[h_ahead, h_back, p_in] + [_layer_resident(w, layer) for w in weights],
        out_specs=[h_out, state_block(N_HIST_A, D_A),
                   state_block(POOL_STATE, D_B), state_block(N_HIST_C, D_C)],
        out_shape=[jax.ShapeDtypeStruct(h.shape, F32),
                   jax.ShapeDtypeStruct((batch, N_HIST_A, D_A), F32),
                   jax.ShapeDtypeStruct((batch, POOL_STATE, D_B), F32),
                   jax.ShapeDtypeStruct((batch, N_HIST_C, D_C), F32)],
        scratch_shapes=[pltpu.VMEM((tile, D_IN), F32),
                        pltpu.VMEM((D_A // LANES, HIST_A + tile, LANES), F32),
                        pltpu.VMEM((D_B // LANES, HIST_P + tile, LANES), F32),
                        pltpu.VMEM((D_C // LANES, HIST_C + tile, LANES), F32),
                        pltpu.VMEM((tile, D_B), BF16),
                        pltpu.VMEM((tile, D_MODEL), BF16),
                        pltpu.VMEM((tile, D_MODEL), BF16),
                        pltpu.VMEM((tile, D_MODEL), F32),
                        pltpu.VMEM((tile, D_MODEL), BF16),
                        pltpu.VMEM((2, tile, FF_SLICE), BF16)],
        compiler_params=pltpu.CompilerParams(
            dimension_semantics=("arbitrary",),
            vmem_limit_bytes=VMEM_LIMIT_BYTES),
        name="prompt_layer",
    )(h, h, p_all, *weights)


def _sample_layers(h, p_all, st_a, st_p, st_c, weights):
    depth = p_all.shape[0]
    rows = h.shape[0]
    n_ff = D_FF // FF_SLICE
    vecs, w_in, caw, pool_bd, ccw, w_out, w_up, w_down, w_pg, w_pp = weights

    def per_layer(arr, **kwargs):
        zeros = (0,) * (arr.ndim - 1)
        return pl.BlockSpec((None,) + arr.shape[1:], lambda l, j: (l,) + zeros, **kwargs)

    once = dict(pipeline_mode=pl.Buffered(1))
    return pl.pallas_call(
        functools.partial(_sample_kernel, pos0=PAST_LEN),
        grid=(depth, n_ff),
        in_specs=[_resident(), per_layer(p_all), per_layer(st_a, **once), per_layer(st_p, **once),
                  per_layer(st_c, **once), per_layer(vecs), per_layer(w_in), per_layer(caw), per_layer(pool_bd),
                  per_layer(ccw), per_layer(w_out),
                  pl.BlockSpec((None, D_MODEL, FF_SLICE), lambda l, j: (l, 0, j)),
                  pl.BlockSpec((None, FF_SLICE, D_MODEL), lambda l, j: (l, j, 0)),
                  per_layer(w_pg), per_layer(w_pp)],
        out_specs=[_resident(), per_layer(st_a), per_layer(st_p), per_layer(st_c)],
        out_shape=[jax.ShapeDtypeStruct(h.shape, F32),
                   jax.ShapeDtypeStruct(st_a.shape, F32),
                   jax.ShapeDtypeStruct(st_p.shape, F32),
                   jax.ShapeDtypeStruct(st_c.shape, F32)],
        scratch_shapes=[pltpu.VMEM((rows, D_MODEL), BF16), pltpu.VMEM((rows, D_MODEL), F32),
                        pltpu.VMEM((rows, D_MODEL), F32), pltpu.VMEM((rows, D_MODEL), BF16)],
        compiler_params=pltpu.CompilerParams(
            dimension_semantics=("arbitrary", "arbitrary"),
            vmem_limit_bytes=VMEM_LIMIT_BYTES),
        name="sample_layers",
    )(h, p_all, st_a, st_p, st_c, *weights)


def _pool_block_diag(pool_w):
    depth, groups, c, _ = pool_w.shape
    eye = jnp.eye(groups, dtype=pool_w.dtype)
    return (eye[None, :, None, :, None] * pool_w[:, :, :, None, :]).reshape(depth, groups * c, groups * c)


def _history_major(x):
    return jnp.transpose(x, (0, 2, 1, 3))


def kernel(x_prompt, x_sample, state_conv_a, state_pool, state_conv_c, p_prompt, p_sample, norm_mix_g, w_in, conv_a_w, conv_a_b, ln_a_g, ln_a_b, pool_w, pool_scale, conv_c_w, w_out, norm_mlp_g, w_up, w_down, norm_ple_g, w_ple_gate, w_ple_proj, final_norm_g):
    depth = w_in.shape[0]
    dec_batch, dec_seq, _ = x_sample.shape
    assert dec_seq == 1
    hp = x_prompt
    hs = x_sample.reshape(dec_batch, D_MODEL)
    ps = p_sample.reshape(depth, dec_batch, D_PLE)
    sa, sp, sc = (_history_major(x) for x in (state_conv_a, state_pool, state_conv_c))
    vectors = dict(g_mix=norm_mix_g, g_mlp=norm_mlp_g, g_ple=norm_ple_g,
                   g_fin=jnp.broadcast_to(final_norm_g, (depth, D_MODEL)),
                   cab=conv_a_b, lng=ln_a_g, lnb=ln_a_b, pscale=pool_scale)
    vecs = jnp.concatenate([vectors[name] for name, _ in _VEC_SIZES], axis=-1).reshape(depth, 1, _VEC_WIDTH)
    weights = (vecs, w_in.astype(BF16), conv_a_w, _pool_block_diag(pool_w).astype(BF16), conv_c_w,
               w_out.astype(BF16), w_up.astype(BF16), w_down.astype(BF16),
               w_ple_gate.astype(BF16), w_ple_proj.astype(BF16))
    prompt_states = []
    for i in range(depth):
        hp, *states = _prompt_layer(hp, p_prompt, i, weights, final=i == depth - 1)
        prompt_states.append(states)
        if i == 0:
            hs, *sample_states = _sample_layers(hs, ps, sa, sp, sc, weights)
    return (hp, hs.reshape(x_sample.shape), *(jnp.stack(x) for x in zip(*prompt_states)),
            *(_history_major(x) for x in sample_states))
```

```python
import functools

import jax
import jax.numpy as jnp
from jax import lax
from jax.experimental import pallas as pl
from jax.experimental.pallas import tpu as pltpu

D_MODEL = 1024
D_A = 384
D_B = 256
D_C = 384
D_IN = 2 * D_A + D_B + 3 * D_C
D_FF = 4096
D_PLE = 256
POOL_WINDOWS = (2, 4, 8, 16)
POOL_GROUP_DIM = D_B // len(POOL_WINDOWS)
POOL_STATE = 15
CONV_A_WIDTH = 31
CONV_C_WIDTH = 3
N_HIST_A = CONV_A_WIDTH - 1
N_HIST_C = CONV_C_WIDTH - 1
PAST_LEN = 16384
EPS = 1e-6

_O_AV = 0
_O_AG = D_A
_O_UB = 2 * D_A
_O_CB = _O_UB + D_B
_O_CC = _O_CB + D_C
_O_CX = _O_CC + D_C
_M_A = 0
_M_B = D_A
_M_C = D_A + D_B

LANES = 128
HIST_A = 32
HIST_P = 16
HIST_C = 8
ROW_CHUNK = 64
SAMPLE_ROW_CHUNK = 32
FF_SLICE = 1024
SEQ_TILE = 512
VMEM_LIMIT_BYTES = 62 * 1024 * 1024

BF16 = jnp.bfloat16
F32 = jnp.float32


_VEC_SIZES = (("g_mix", D_MODEL), ("g_mlp", D_MODEL), ("g_ple", D_MODEL), ("g_fin", D_MODEL),
              ("cab", D_A), ("lng", D_A), ("lnb", D_A), ("pscale", D_B))
_VEC_LAYOUT = {}
_VEC_WIDTH = 0
for _name, _size in _VEC_SIZES:
    _VEC_LAYOUT[_name] = (_VEC_WIDTH, _size)
    _VEC_WIDTH += _size


class _Cols:
    def __init__(self, ref, start, size):
        self.ref, self.start, self.size = ref, start, size

    def __getitem__(self, idx):
        if idx is Ellipsis:
            return self.ref[:, self.start:self.start + self.size]
        rows, cols = idx
        lo = cols.start or 0
        hi = self.size if cols.stop is None else cols.stop
        return self.ref[rows, self.start + lo:self.start + hi]


def _vector_params(vecs):
    return tuple(_Cols(vecs, *_VEC_LAYOUT[name]) for name, _ in _VEC_SIZES)


def _rms_norm(x, g):
    ms = jnp.mean(x * x, axis=-1, keepdims=True)
    return x * lax.rsqrt(ms + EPS) * g


def _dot(a, b):
    return jnp.dot(a, b, preferred_element_type=F32)


def _layer_norm_silu(ya, g, b):
    mu = jnp.mean(ya, axis=-1, keepdims=True)
    xc = ya - mu
    var = jnp.mean(xc * xc, axis=-1, keepdims=True)
    y = xc * lax.rsqrt(var + EPS) * g + b
    return y * jax.nn.sigmoid(y)


def _pool_select(s_small, s_big, w_small, w_big, pos1):
    lane = lax.broadcasted_iota(jnp.int32, s_small.shape, 1)
    cnt_small = jnp.minimum(pos1, w_small).astype(F32)
    cnt_big = jnp.minimum(pos1, w_big).astype(F32)
    return jnp.where(lane < POOL_GROUP_DIM, s_small / cnt_small, s_big / cnt_big)


def _lane_cols(offset, c):
    return slice(offset + c * LANES, offset + (c + 1) * LANES)


def _gated_embedding(h2, p, g_ple, w_pg, w_pp, g_fin, final):
    r3 = _rms_norm(h2, g_ple[...]).astype(BF16)
    gate = jax.nn.sigmoid(_dot(r3, w_pg[...]))
    h3 = h2 + gate * _dot(p.astype(BF16), w_pp[...])
    if final:
        h3 = _rms_norm(h3, g_fin[...])
    return h3


def _prompt_kernel(hn_ref, hb_ref, p_ref, vecs, w_in, caw, pool_bd, ccw, w_out, w_up_hbm, w_down_hbm, w_pg, w_pp,
                   *rest, tile, n_tiles, tiles_per_seq, layer, final):
    (out_ref, na_ref, np_ref, nc_ref,
     zbuf, gbuf, ubuf, vbuf, poolbuf, mixbuf, nbuf, h2buf, mbuf, actbuf, w_up, w_down, w_sem) = rest
    g_mix, g_mlp, g_ple, g_fin, cab, lng, lnb, pscale = _vector_params(vecs)
    s = pl.program_id(0)
    t = s % tiles_per_seq

    def mlp_weight_copies():
        return (pltpu.make_async_copy(w_up_hbm.at[layer], w_up, w_sem.at[0]),
                pltpu.make_async_copy(w_down_hbm.at[layer], w_down, w_sem.at[1]))

    @pl.when(s == 0)
    def _():
        for copy in mlp_weight_copies():
            copy.start()

    @pl.when(s == 1)
    def _():
        for copy in mlp_weight_copies():
            copy.wait()

    def shifted(buf, c, start):
        return buf[c, pl.ds(start, ROW_CHUNK, stride=1), :]

    def mixers(r0):
        rows = pl.ds(r0, ROW_CHUNK)
        for c in range(D_A // LANES):
            glu = zbuf[rows, _lane_cols(_O_AV, c)] * jax.nn.sigmoid(zbuf[rows, _lane_cols(_O_AG, c)])
            gbuf[c, pl.ds(r0 + HIST_A, ROW_CHUNK), :] = glu
        for c in range(D_B // LANES):
            ubuf[c, pl.ds(r0 + HIST_P, ROW_CHUNK), :] = zbuf[rows, _lane_cols(_O_UB, c)]
        for c in range(D_C // LANES):
            vbuf[c, pl.ds(r0 + HIST_C, ROW_CHUNK), :] = (
                zbuf[rows, _lane_cols(_O_CC, c)] * zbuf[rows, _lane_cols(_O_CX, c)])

        conv_a = []
        for c in range(D_A // LANES):
            acc = jnp.broadcast_to(cab[:, _lane_cols(0, c)], (ROW_CHUNK, LANES))
            for k in range(CONV_A_WIDTH):
                acc = acc + caw[k:k + 1, _lane_cols(0, c)] * shifted(gbuf, c, r0 + HIST_A - N_HIST_A + k)
            conv_a.append(acc)
        out_a = _layer_norm_silu(jnp.concatenate(conv_a, axis=-1), lng[...], lnb[...])
        mixbuf[rows, _M_A:_M_A + D_A] = out_a.astype(BF16)

        pos1 = (t * tile + r0 + 1) + lax.broadcasted_iota(jnp.int32, (ROW_CHUNK, LANES), 0)
        u_lo = [shifted(ubuf, 0, r0 + HIST_P - j) for j in range(4)]
        s2 = u_lo[0] + u_lo[1]
        s4 = s2 + (u_lo[2] + u_lo[3])
        pooled_lo = _pool_select(s2, s4, 2, 4, pos1) - u_lo[0]
        u_hi = [shifted(ubuf, 1, r0 + HIST_P - j) for j in range(16)]
        s8 = u_hi[0]
        for j in range(1, 8):
            s8 = s8 + u_hi[j]
        s16 = s8
        for j in range(8, 16):
            s16 = s16 + u_hi[j]
        pooled_hi = _pool_select(s8, s16, 8, 16, pos1) - u_hi[0]
        poolbuf[rows, 0:LANES] = pooled_lo.astype(BF16)
        poolbuf[rows, LANES:2 * LANES] = pooled_hi.astype(BF16)

        for c in range(D_C // LANES):
            conv_c = ccw[2:3, _lane_cols(0, c)] * shifted(vbuf, c, r0 + HIST_C)
            conv_c = conv_c + ccw[1:2, _lane_cols(0, c)] * shifted(vbuf, c, r0 + HIST_C - 1)
            conv_c = conv_c + ccw[0:1, _lane_cols(0, c)] * shifted(vbuf, c, r0 + HIST_C - 2)
            out_c = zbuf[rows, _lane_cols(_O_CB, c)] * conv_c
            mixbuf[rows, _lane_cols(_M_C, c)] = out_c.astype(BF16)

    n_ff = D_FF // FF_SLICE
    chunks_per_slice = tile // ROW_CHUNK // n_ff

    def mixer_blocks(j):
        for q in range(chunks_per_slice):
            mixers((j * chunks_per_slice + q) * ROW_CHUNK)

    def up_act(j):
        up = _dot(mbuf[...], w_up[:, j * FF_SLICE:(j + 1) * FF_SLICE])
        actbuf[j % 2] = jnp.square(jnp.maximum(up, 0.0)).astype(BF16)

    def step(front, back):
        if front:
            @pl.when(t == 0)
            def _():
                gbuf[:, 0:HIST_A, :] = jnp.zeros((D_A // LANES, HIST_A, LANES), F32)
                ubuf[:, 0:HIST_P, :] = jnp.zeros((D_B // LANES, HIST_P, LANES), F32)
                vbuf[:, 0:HIST_C, :] = jnp.zeros((D_C // LANES, HIST_C, LANES), F32)

        if front and not back:
            nbuf[...] = _rms_norm(hb_ref[...], g_mix[...]).astype(BF16)
        if back:
            h1 = hb_ref[...] + _dot(mixbuf[...], w_out[...])
        if front:
            zbuf[...] = _dot(nbuf[...], w_in[...])
            nbuf[...] = _rms_norm(hn_ref[...], g_mix[...]).astype(BF16)
        if back:
            h2buf[...] = h1
            mbuf[...] = _rms_norm(h1, g_mlp[...]).astype(BF16)
            up_act(0)

        for j in range(n_ff):
            if front:
                mixer_blocks(j)
            if back:
                down = _dot(actbuf[j % 2], w_down[j * FF_SLICE:(j + 1) * FF_SLICE, :])
                if j + 1 < n_ff:
                    up_act(j + 1)
                h2buf[...] += down

        if front:
            out_b = _dot(poolbuf[...], pool_bd[...]) * pscale[...]
            mixbuf[:, _M_B:_M_B + D_B] = out_b.astype(BF16)
            gbuf[:, 0:HIST_A, :] = gbuf[:, tile:tile + HIST_A, :]
            ubuf[:, 0:HIST_P, :] = ubuf[:, tile:tile + HIST_P, :]
            vbuf[:, 0:HIST_C, :] = vbuf[:, tile:tile + HIST_C, :]
        if back:
            out_ref[...] = _gated_embedding(h2buf[...], p_ref[...], g_ple, w_pg, w_pp, g_fin, final)

        if front:
            @pl.when(t == tiles_per_seq - 1)
            def _():
                for c in range(D_A // LANES):
                    na_ref[:, _lane_cols(0, c)] = gbuf[c, HIST_A + tile - N_HIST_A:HIST_A + tile, :]
                for c in range(D_B // LANES):
                    np_ref[:, _lane_cols(0, c)] = ubuf[c, HIST_P + tile - POOL_STATE:HIST_P + tile, :]
                for c in range(D_C // LANES):
                    nc_ref[:, _lane_cols(0, c)] = vbuf[c, HIST_C + tile - N_HIST_C:HIST_C + tile, :]

    pl.when(s == 0)(lambda: step(True, False))
    pl.when((s > 0) & (s < n_tiles))(lambda: step(True, True))
    pl.when(s == n_tiles)(lambda: step(False, True))


def _sample_kernel(h_ref, p_ref, sa_ref, sp_ref, sc_ref, vecs, w_in, caw, pool_bd, ccw, w_out, w_up, w_down, w_pg, w_pp,
                   out_ref, na_ref, np_ref, nc_ref, mixbuf, hbuf, h2buf, mbuf, *, pos0):
    g_mix, g_mlp, g_ple, g_fin, cab, lng, lnb, pscale = _vector_params(vecs)
    layer = pl.program_id(0)
    j = pl.program_id(1)

    @pl.when(j == 0)
    def _():
        @pl.when(layer == 0)
        def _():
            hbuf[...] = h_ref[...]

        h = hbuf[...]
        rows = h.shape[0]
        z = _dot(_rms_norm(h, g_mix[...]).astype(BF16), w_in[...])
        u = z[:, _O_UB:_O_UB + D_B]
        v = z[:, _O_CC:_O_CC + D_C] * z[:, _O_CX:_O_CX + D_C]
        glu = z[:, _O_AV:_O_AV + D_A] * jax.nn.sigmoid(z[:, _O_AG:_O_AG + D_A])

        for r0 in range(0, rows, SAMPLE_ROW_CHUNK):
            blk = slice(r0, r0 + SAMPLE_ROW_CHUNK)
            acc = cab[...] + caw[N_HIST_A:N_HIST_A + 1, :] * glu[blk]
            for k in range(N_HIST_A):
                acc = acc + caw[k:k + 1, :] * sa_ref[k, blk, :]
            mixbuf[blk, _M_A:_M_A + D_A] = _layer_norm_silu(acc, lng[...], lnb[...]).astype(BF16)
        na_ref[0:N_HIST_A - 1] = sa_ref[1:N_HIST_A]
        na_ref[N_HIST_A - 1] = glu

        pos1 = jnp.full((rows, LANES), pos0 + 1, jnp.int32)
        u_lo = u[:, 0:LANES]
        s2 = u_lo + sp_ref[POOL_STATE - 1, :, 0:LANES]
        s4 = s2 + (sp_ref[POOL_STATE - 2, :, 0:LANES] + sp_ref[POOL_STATE - 3, :, 0:LANES])
        pooled_lo = _pool_select(s2, s4, 2, 4, pos1) - u_lo
        u_hi = u[:, LANES:2 * LANES]
        s8 = u_hi
        for i in range(1, 8):
            s8 = s8 + sp_ref[POOL_STATE - i, :, LANES:2 * LANES]
        s16 = s8
        for i in range(8, 16):
            s16 = s16 + sp_ref[POOL_STATE - i, :, LANES:2 * LANES]
        pooled_hi = _pool_select(s8, s16, 8, 16, pos1) - u_hi
        pooled = jnp.concatenate([pooled_lo, pooled_hi], axis=-1).astype(BF16)
        mixbuf[:, _M_B:_M_B + D_B] = (_dot(pooled, pool_bd[...]) * pscale[...]).astype(BF16)
        np_ref[0:POOL_STATE - 1] = sp_ref[1:POOL_STATE]
        np_ref[POOL_STATE - 1] = u

        conv_c = ccw[2:3, :] * v + ccw[1:2, :] * sc_ref[1] + ccw[0:1, :] * sc_ref[0]
        mixbuf[:, _M_C:_M_C + D_C] = (z[:, _O_CB:_O_CB + D_C] * conv_c).astype(BF16)
        nc_ref[0] = sc_ref[1]
        nc_ref[1] = v

        h1 = h + _dot(mixbuf[...], w_out[...])
        h2buf[...] = h1
        mbuf[...] = _rms_norm(h1, g_mlp[...]).astype(BF16)

    act = jnp.square(jnp.maximum(_dot(mbuf[...], w_up[...]), 0.0)).astype(BF16)
    h2buf[...] += _dot(act, w_down[...])

    @pl.when(j == pl.num_programs(1) - 1)
    def _():
        h3 = _gated_embedding(h2buf[...], p_ref[...], g_ple, w_pg, w_pp, g_fin, False)
        hbuf[...] = h3

        @pl.when(layer == pl.num_programs(0) - 1)
        def _():
            out_ref[...] = _rms_norm(h3, g_fin[...])


_MLP_WEIGHT_POSITIONS = (6, 7)


def _resident():
    return pl.BlockSpec(memory_space=pltpu.VMEM)


def _layer_resident(arr, layer):
    zeros = (0,) * (arr.ndim - 1)
    return pl.BlockSpec((None,) + arr.shape[1:], lambda s: (layer,) + zeros, pipeline_mode=pl.Buffered(1))


def _prompt_layer(h, p_all, layer, weights, *, final):
    batch, seq, _ = h.shape
    tile = min(SEQ_TILE, seq)
    assert seq % tile == 0 and tile % (ROW_CHUNK * (D_FF // FF_SLICE)) == 0 and tile >= HIST_A
    assert weights[6].shape[1:] == (D_MODEL, D_FF) and weights[7].shape[1:] == (D_FF, D_MODEL)
    tiles_per_seq = seq // tile
    n_tiles = batch * tiles_per_seq

    def front(s):
        i = jnp.minimum(s, n_tiles - 1)
        return i // tiles_per_seq, i % tiles_per_seq

    def back(s):
        i = jnp.maximum(s - 1, 0)
        return i // tiles_per_seq, i % tiles_per_seq

    def ahead(s):
        i = jnp.minimum(s + 1, n_tiles - 1)
        return i // tiles_per_seq, i % tiles_per_seq

    h_ahead = pl.BlockSpec((None, tile, D_MODEL), lambda s: (*ahead(s), 0))
    h_back = pl.BlockSpec((None, tile, D_MODEL), lambda s: (*back(s), 0))
    p_in = pl.BlockSpec((None, None, tile, D_PLE), lambda s: (layer, *back(s), 0))
    h_out = pl.BlockSpec((None, tile, D_MODEL), lambda s: (*back(s), 0))
    state_block = lambda rows, width: pl.BlockSpec((None, rows, width), lambda s: (front(s)[0], 0, 0))
    return pl.pallas_call(
        functools.partial(_prompt_kernel, tile=tile, n_tiles=n_tiles, tiles_per_seq=tiles_per_seq,
                          layer=layer, final=final),
        grid=(n_tiles + 1,),
        in_specs=[h_ahead, h_back, p_in] + [
            pl.BlockSpec(memory_space=pl.ANY) if i in _MLP_WEIGHT_POSITIONS else _layer_resident(w, layer)
            for i, w in enumerate(weights)],
        out_specs=[h_out, state_block(N_HIST_A, D_A),
                   state_block(POOL_STATE, D_B), state_block(N_HIST_C, D_C)],
        out_shape=[jax.ShapeDtypeStruct(h.shape, F32),
                   jax.ShapeDtypeStruct((batch, N_HIST_A, D_A), F32),
                   jax.ShapeDtypeStruct((batch, POOL_STATE, D_B), F32),
                   jax.ShapeDtypeStruct((batch, N_HIST_C, D_C), F32)],
        scratch_shapes=[pltpu.VMEM((tile, D_IN), F32),
                        pltpu.VMEM((D_A // LANES, HIST_A + tile, LANES), F32),
                        pltpu.VMEM((D_B // LANES, HIST_P + tile, LANES), F32),
                        pltpu.VMEM((D_C // LANES, HIST_C + tile, LANES), F32),
                        pltpu.VMEM((tile, D_B), BF16),
                        pltpu.VMEM((tile, D_MODEL), BF16),
                        pltpu.VMEM((tile, D_MODEL), BF16),
                        pltpu.VMEM((tile, D_MODEL), F32),
                        pltpu.VMEM((tile, D_MODEL), BF16),
                        pltpu.VMEM((2, tile, FF_SLICE), BF16),
                        pltpu.VMEM((D_MODEL, D_FF), BF16),
                        pltpu.VMEM((D_FF, D_MODEL), BF16),
                        pltpu.SemaphoreType.DMA((2,))],
        compiler_params=pltpu.CompilerParams(
            dimension_semantics=("arbitrary",),
            vmem_limit_bytes=VMEM_LIMIT_BYTES),
        name="prompt_layer",
    )(h, h, p_all, *weights)


def _sample_layers(h, p_all, st_a, st_p, st_c, weights):
    depth = p_all.shape[0]
    rows = h.shape[0]
    n_ff = D_FF // FF_SLICE
    vecs, w_in, caw, pool_bd, ccw, w_out, w_up, w_down, w_pg, w_pp = weights

    def per_layer(arr, **kwargs):
        zeros = (0,) * (arr.ndim - 1)
        return pl.BlockSpec((None,) + arr.shape[1:], lambda l, j: (l,) + zeros, **kwargs)

    once = dict(pipeline_mode=pl.Buffered(1))
    return pl.pallas_call(
        functools.partial(_sample_kernel, pos0=PAST_LEN),
        grid=(depth, n_ff),
        in_specs=[_resident(), per_layer(p_all), per_layer(st_a, **once), per_layer(st_p, **once),
                  per_layer(st_c, **once), per_layer(vecs), per_layer(w_in), per_layer(caw), per_layer(pool_bd),
                  per_layer(ccw), per_layer(w_out),
                  pl.BlockSpec((None, D_MODEL, FF_SLICE), lambda l, j: (l, 0, j)),
                  pl.BlockSpec((None, FF_SLICE, D_MODEL), lambda l, j: (l, j, 0)),
                  per_layer(w_pg), per_layer(w_pp)],
        out_specs=[_resident(), per_layer(st_a), per_layer(st_p), per_layer(st_c)],
        out_shape=[jax.ShapeDtypeStruct(h.shape, F32),
                   jax.ShapeDtypeStruct(st_a.shape, F32),
                   jax.ShapeDtypeStruct(st_p.shape, F32),
                   jax.ShapeDtypeStruct(st_c.shape, F32)],
        scratch_shapes=[pltpu.VMEM((rows, D_MODEL), BF16), pltpu.VMEM((rows, D_MODEL), F32),
                        pltpu.VMEM((rows, D_MODEL), F32), pltpu.VMEM((rows, D_MODEL), BF16)],
        compiler_params=pltpu.CompilerParams(
            dimension_semantics=("arbitrary", "arbitrary"),
            vmem_limit_bytes=VMEM_LIMIT_BYTES),
        name="sample_layers",
    )(h, p_all, st_a, st_p, st_c, *weights)


def _pool_block_diag(pool_w):
    depth, groups, c, _ = pool_w.shape
    eye = jnp.eye(groups, dtype=pool_w.dtype)
    return (eye[None, :, None, :, None] * pool_w[:, :, :, None, :]).reshape(depth, groups * c, groups * c)


def _history_major(x):
    return jnp.transpose(x, (0, 2, 1, 3))


def kernel(x_prompt, x_sample, state_conv_a, state_pool, state_conv_c, p_prompt, p_sample, norm_mix_g, w_in, conv_a_w, conv_a_b, ln_a_g, ln_a_b, pool_w, pool_scale, conv_c_w, w_out, norm_mlp_g, w_up, w_down, norm_ple_g, w_ple_gate, w_ple_proj, final_norm_g):
    depth = w_in.shape[0]
    dec_batch, dec_seq, _ = x_sample.shape
    assert dec_seq == 1
    hp = x_prompt
    hs = x_sample.reshape(dec_batch, D_MODEL)
    ps = p_sample.reshape(depth, dec_batch, D_PLE)
    sa, sp, sc = (_history_major(x) for x in (state_conv_a, state_pool, state_conv_c))
    vectors = dict(g_mix=norm_mix_g, g_mlp=norm_mlp_g, g_ple=norm_ple_g,
                   g_fin=jnp.broadcast_to(final_norm_g, (depth, D_MODEL)),
                   cab=conv_a_b, lng=ln_a_g, lnb=ln_a_b, pscale=pool_scale)
    vecs = jnp.concatenate([vectors[name] for name, _ in _VEC_SIZES], axis=-1).reshape(depth, 1, _VEC_WIDTH)
    weights = (vecs, w_in.astype(BF16), conv_a_w, _pool_block_diag(pool_w).astype(BF16), conv_c_w,
               w_out.astype(BF16), w_up.astype(BF16), w_down.astype(BF16),
               w_ple_gate.astype(BF16), w_ple_proj.astype(BF16))
    prompt_states = []
    for i in range(depth):
        hp, *states = _prompt_layer(hp, p_prompt, i, weights, final=i == depth - 1)
        prompt_states.append(states)
    hs, *sample_states = _sample_layers(hs, ps, sa, sp, sc, weights)
    return (hp, hs.reshape(x_sample.shape), *(jnp.stack(x) for x in zip(*prompt_states)),
            *(_history_major(x) for x in sample_states))
```

```python
import functools

import jax
import jax.numpy as jnp
from jax import lax
from jax.experimental import pallas as pl
from jax.experimental.pallas import tpu as pltpu

D_MODEL = 1024
D_A = 384
D_B = 256
D_C = 384
D_IN = 2 * D_A + D_B + 3 * D_C
D_FF = 4096
D_PLE = 256
POOL_WINDOWS = (2, 4, 8, 16)
POOL_GROUP_DIM = D_B // len(POOL_WINDOWS)
POOL_STATE = 15
CONV_A_WIDTH = 31
CONV_C_WIDTH = 3
N_HIST_A = CONV_A_WIDTH - 1
N_HIST_C = CONV_C_WIDTH - 1
PAST_LEN = 16384
EPS = 1e-6

_O_AV = 0
_O_AG = D_A
_O_UB = 2 * D_A
_O_CB = _O_UB + D_B
_O_CC = _O_CB + D_C
_O_CX = _O_CC + D_C
_M_A = 0
_M_B = D_A
_M_C = D_A + D_B

LANES = 128
HIST_A = 32
HIST_P = 16
HIST_C = 8
ROW_CHUNK = 64
SAMPLE_ROW_CHUNK = 32
FF_SLICE = 1024
SEQ_TILE = 512
VMEM_LIMIT_BYTES = 62 * 1024 * 1024

BF16 = jnp.bfloat16
F32 = jnp.float32


_VEC_SIZES = (("g_mix", D_MODEL), ("g_mlp", D_MODEL), ("g_ple", D_MODEL), ("g_fin", D_MODEL),
              ("cab", D_A), ("lng", D_A), ("lnb", D_A), ("pscale", D_B))
_VEC_LAYOUT = {}
_VEC_WIDTH = 0
for _name, _size in _VEC_SIZES:
    _VEC_LAYOUT[_name] = (_VEC_WIDTH, _size)
    _VEC_WIDTH += _size


class _Cols:
    def __init__(self, ref, start, size):
        self.ref, self.start, self.size = ref, start, size

    def __getitem__(self, idx):
        if idx is Ellipsis:
            return self.ref[:, self.start:self.start + self.size]
        rows, cols = idx
        lo = cols.start or 0
        hi = self.size if cols.stop is None else cols.stop
        return self.ref[rows, self.start + lo:self.start + hi]


def _vector_params(vecs):
    return tuple(_Cols(vecs, *_VEC_LAYOUT[name]) for name, _ in _VEC_SIZES)


def _rms_norm(x, g):
    ms = jnp.mean(x * x, axis=-1, keepdims=True)
    return x * lax.rsqrt(ms + EPS) * g


def _dot(a, b):
    return jnp.dot(a, b, preferred_element_type=F32)


def _layer_norm_silu(ya, g, b):
    mu = jnp.mean(ya, axis=-1, keepdims=True)
    xc = ya - mu
    var = jnp.mean(xc * xc, axis=-1, keepdims=True)
    y = xc * lax.rsqrt(var + EPS) * g + b
    return y * jax.nn.sigmoid(y)


def _pool_select(s_small, s_big, w_small, w_big, pos1):
    lane = lax.broadcasted_iota(jnp.int32, s_small.shape, 1)
    cnt_small = jnp.minimum(pos1, w_small).astype(F32)
    cnt_big = jnp.minimum(pos1, w_big).astype(F32)
    return jnp.where(lane < POOL_GROUP_DIM, s_small / cnt_small, s_big / cnt_big)


def _lane_cols(offset, c):
    return slice(offset + c * LANES, offset + (c + 1) * LANES)


def _gated_embedding(h2, p, g_ple, w_pg, w_pp, g_fin, final):
    r3 = _rms_norm(h2, g_ple[...]).astype(BF16)
    gate = jax.nn.sigmoid(_dot(r3, w_pg[...]))
    h3 = h2 + gate * _dot(p.astype(BF16), w_pp[...])
    if final:
        h3 = _rms_norm(h3, g_fin[...])
    return h3


def _prompt_kernel(hn_ref, hb_ref, p_ref, vecs, w_in, caw, pool_bd, ccw, w_out, w_up_hbm, w_down_hbm, w_pg, w_pp,
                   *rest, tile, n_tiles, tiles_per_seq, layer, final):
    (out_ref, na_ref, np_ref, nc_ref,
     zbuf, gbuf, ubuf, vbuf, poolbuf, mixbuf, nbuf, h2buf, mbuf, actbuf, w_up, w_down, w_sem) = rest
    g_mix, g_mlp, g_ple, g_fin, cab, lng, lnb, pscale = _vector_params(vecs)
    s = pl.program_id(0)
    t = s % tiles_per_seq

    def mlp_weight_copies():
        return (pltpu.make_async_copy(w_up_hbm.at[layer], w_up, w_sem.at[0]),
                pltpu.make_async_copy(w_down_hbm.at[layer], w_down, w_sem.at[1]))

    @pl.when(s == 0)
    def _():
        for copy in mlp_weight_copies():
            copy.start()

    @pl.when(s == 1)
    def _():
        for copy in mlp_weight_copies():
            copy.wait()

    def shifted(buf, c, start):
        return buf[c, pl.ds(start, ROW_CHUNK, stride=1), :]

    def mixers(r0):
        rows = pl.ds(r0, ROW_CHUNK)
        for c in range(D_A // LANES):
            glu = zbuf[rows, _lane_cols(_O_AV, c)] * jax.nn.sigmoid(zbuf[rows, _lane_cols(_O_AG, c)])
            gbuf[c, pl.ds(r0 + HIST_A, ROW_CHUNK), :] = glu
        for c in range(D_B // LANES):
            ubuf[c, pl.ds(r0 + HIST_P, ROW_CHUNK), :] = zbuf[rows, _lane_cols(_O_UB, c)]
        for c in range(D_C // LANES):
            vbuf[c, pl.ds(r0 + HIST_C, ROW_CHUNK), :] = (
                zbuf[rows, _lane_cols(_O_CC, c)] * zbuf[rows, _lane_cols(_O_CX, c)])

        conv_a = []
        for c in range(D_A // LANES):
            acc = jnp.broadcast_to(cab[:, _lane_cols(0, c)], (ROW_CHUNK, LANES))
            for k in range(CONV_A_WIDTH):
                acc = acc + caw[k:k + 1, _lane_cols(0, c)] * shifted(gbuf, c, r0 + HIST_A - N_HIST_A + k)
            conv_a.append(acc)
        out_a = _layer_norm_silu(jnp.concatenate(conv_a, axis=-1), lng[...], lnb[...])
        mixbuf[rows, _M_A:_M_A + D_A] = out_a.astype(BF16)

        pos1 = (t * tile + r0 + 1) + lax.broadcasted_iota(jnp.int32, (ROW_CHUNK, LANES), 0)
        u_lo = [shifted(ubuf, 0, r0 + HIST_P - j) for j in range(4)]
        s2 = u_lo[0] + u_lo[1]
        s4 = s2 + (u_lo[2] + u_lo[3])
        pooled_lo = _pool_select(s2, s4, 2, 4, pos1) - u_lo[0]
        u_hi = [shifted(ubuf, 1, r0 + HIST_P - j) for j in range(16)]
        s8 = u_hi[0]
        for j in range(1, 8):
            s8 = s8 + u_hi[j]
        s16 = s8
        for j in range(8, 16):
            s16 = s16 + u_hi[j]
        pooled_hi = _pool_select(s8, s16, 8, 16, pos1) - u_hi[0]
        poolbuf[rows, 0:LANES] = pooled_lo.astype(BF16)
        poolbuf[rows, LANES:2 * LANES] = pooled_hi.astype(BF16)

        for c in range(D_C // LANES):
            conv_c = ccw[2:3, _lane_cols(0, c)] * shifted(vbuf, c, r0 + HIST_C)
            conv_c = conv_c + ccw[1:2, _lane_cols(0, c)] * shifted(vbuf, c, r0 + HIST_C - 1)
            conv_c = conv_c + ccw[0:1, _lane_cols(0, c)] * shifted(vbuf, c, r0 + HIST_C - 2)
            out_c = zbuf[rows, _lane_cols(_O_CB, c)] * conv_c
            mixbuf[rows, _lane_cols(_M_C, c)] = out_c.astype(BF16)

    n_ff = D_FF // FF_SLICE
    chunks_per_slice = tile // ROW_CHUNK // n_ff

    def mixer_blocks(j):
        for q in range(chunks_per_slice):
            mixers((j * chunks_per_slice + q) * ROW_CHUNK)

    def up_act(j):
        up = _dot(mbuf[...], w_up[:, j * FF_SLICE:(j + 1) * FF_SLICE])
        actbuf[j % 2] = jnp.square(jnp.maximum(up, 0.0)).astype(BF16)

    def step(front, back):
        if front:
            @pl.when(t == 0)
            def _():
                gbuf[:, 0:HIST_A, :] = jnp.zeros((D_A // LANES, HIST_A, LANES), F32)
                ubuf[:, 0:HIST_P, :] = jnp.zeros((D_B // LANES, HIST_P, LANES), F32)
                vbuf[:, 0:HIST_C, :] = jnp.zeros((D_C // LANES, HIST_C, LANES), F32)

        if front and not back:
            nbuf[...] = _rms_norm(hb_ref[...], g_mix[...]).astype(BF16)
        if back:
            h1 = hb_ref[...] + _dot(mixbuf[...], w_out[...])
        if front:
            zbuf[...] = _dot(nbuf[...], w_in[...])
            nbuf[...] = _rms_norm(hn_ref[...], g_mix[...]).astype(BF16)
        if back:
            h2buf[...] = h1
            mbuf[...] = _rms_norm(h1, g_mlp[...]).astype(BF16)
            up_act(0)

        for j in range(n_ff):
            if front:
                mixer_blocks(j)
            if back:
                down = _dot(actbuf[j % 2], w_down[j * FF_SLICE:(j + 1) * FF_SLICE, :])
                if j + 1 < n_ff:
                    up_act(j + 1)
                h2buf[...] += down

        if front:
            out_b = _dot(poolbuf[...], pool_bd[...]) * pscale[...]
            mixbuf[:, _M_B:_M_B + D_B] = out_b.astype(BF16)
            gbuf[:, 0:HIST_A, :] = gbuf[:, tile:tile + HIST_A, :]
            ubuf[:, 0:HIST_P, :] = ubuf[:, tile:tile + HIST_P, :]
            vbuf[:, 0:HIST_C, :] = vbuf[:, tile:tile + HIST_C, :]
        if back:
            out_ref[...] = _gated_embedding(h2buf[...], p_ref[...], g_ple, w_pg, w_pp, g_fin, final)

        if front:
            @pl.when(t == tiles_per_seq - 1)
            def _():
                for c in range(D_A // LANES):
                    na_ref[:, _lane_cols(0, c)] = gbuf[c, HIST_A + tile - N_HIST_A:HIST_A + tile, :]
                for c in range(D_B // LANES):
                    np_ref[:, _lane_cols(0, c)] = ubuf[c, HIST_P + tile - POOL_STATE:HIST_P + tile, :]
                for c in range(D_C // LANES):
                    nc_ref[:, _lane_cols(0, c)] = vbuf[c, HIST_C + tile - N_HIST_C:HIST_C + tile, :]

    pl.when(s == 0)(lambda: step(True, False))
    pl.when((s > 0) & (s < n_tiles))(lambda: step(True, True))
    pl.when(s == n_tiles)(lambda: step(False, True))


def _sample_kernel(h_ref, p_ref, sa_ref, sp_ref, sc_ref, vecs, w_in, caw, pool_bd, ccw, w_out, w_up, w_down, w_pg, w_pp,
                   out_ref, na_ref, np_ref, nc_ref, mixbuf, hbuf, h2buf, mbuf, *, pos0):
    g_mix, g_mlp, g_ple, g_fin, cab, lng, lnb, pscale = _vector_params(vecs)
    layer = pl.program_id(0)
    j = pl.program_id(1)

    @pl.when(j == 0)
    def _():
        @pl.when(layer == 0)
        def _():
            hbuf[...] = h_ref[:, 0, :]

        h = hbuf[...]
        rows = h.shape[0]
        z = _dot(_rms_norm(h, g_mix[...]).astype(BF16), w_in[...])
        u = z[:, _O_UB:_O_UB + D_B]
        v = z[:, _O_CC:_O_CC + D_C] * z[:, _O_CX:_O_CX + D_C]
        glu = z[:, _O_AV:_O_AV + D_A] * jax.nn.sigmoid(z[:, _O_AG:_O_AG + D_A])

        for r0 in range(0, rows, SAMPLE_ROW_CHUNK):
            blk = slice(r0, r0 + SAMPLE_ROW_CHUNK)
            acc = cab[...] + caw[N_HIST_A:N_HIST_A + 1, :] * glu[blk]
            for k in range(N_HIST_A):
                acc = acc + caw[k:k + 1, :] * sa_ref[k, blk, :]
            mixbuf[blk, _M_A:_M_A + D_A] = _layer_norm_silu(acc, lng[...], lnb[...]).astype(BF16)
        na_ref[0:N_HIST_A - 1] = sa_ref[1:N_HIST_A]
        na_ref[N_HIST_A - 1] = glu

        pos1 = jnp.full((rows, LANES), pos0 + 1, jnp.int32)
        u_lo = u[:, 0:LANES]
        s2 = u_lo + sp_ref[POOL_STATE - 1, :, 0:LANES]
        s4 = s2 + (sp_ref[POOL_STATE - 2, :, 0:LANES] + sp_ref[POOL_STATE - 3, :, 0:LANES])
        pooled_lo = _pool_select(s2, s4, 2, 4, pos1) - u_lo
        u_hi = u[:, LANES:2 * LANES]
        s8 = u_hi
        for i in range(1, 8):
            s8 = s8 + sp_ref[POOL_STATE - i, :, LANES:2 * LANES]
        s16 = s8
        for i in range(8, 16):
            s16 = s16 + sp_ref[POOL_STATE - i, :, LANES:2 * LANES]
        pooled_hi = _pool_select(s8, s16, 8, 16, pos1) - u_hi
        pooled = jnp.concatenate([pooled_lo, pooled_hi], axis=-1).astype(BF16)
        mixbuf[:, _M_B:_M_B + D_B] = (_dot(pooled, pool_bd[...]) * pscale[...]).astype(BF16)
        np_ref[0:POOL_STATE - 1] = sp_ref[1:POOL_STATE]
        np_ref[POOL_STATE - 1] = u

        conv_c = ccw[2:3, :] * v + ccw[1:2, :] * sc_ref[1] + ccw[0:1, :] * sc_ref[0]
        mixbuf[:, _M_C:_M_C + D_C] = (z[:, _O_CB:_O_CB + D_C] * conv_c).astype(BF16)
        nc_ref[0] = sc_ref[1]
        nc_ref[1] = v

        h1 = h + _dot(mixbuf[...], w_out[...])
        h2buf[...] = h1
        mbuf[...] = _rms_norm(h1, g_mlp[...]).astype(BF16)

    act = jnp.square(jnp.maximum(_dot(mbuf[...], w_up[...]), 0.0)).astype(BF16)
    h2buf[...] += _dot(act, w_down[...])

    @pl.when(j == pl.num_programs(1) - 1)
    def _():
        h3 = _gated_embedding(h2buf[...], p_ref[...], g_ple, w_pg, w_pp, g_fin, False)
        hbuf[...] = h3

        @pl.when(layer == pl.num_programs(0) - 1)
        def _():
            out_ref[:, 0, :] = _rms_norm(h3, g_fin[...])


_MLP_WEIGHT_POSITIONS = (6, 7)


def _resident():
    return pl.BlockSpec(memory_space=pltpu.VMEM)


def _layer_resident(arr, layer):
    zeros = (0,) * (arr.ndim - 1)
    return pl.BlockSpec((None,) + arr.shape[1:], lambda s: (layer,) + zeros, pipeline_mode=pl.Buffered(1))


def _prompt_layer(h, p_all, layer, weights, *, final):
    batch, seq, _ = h.shape
    tile = min(SEQ_TILE, seq)
    assert seq % tile == 0 and tile % (ROW_CHUNK * (D_FF // FF_SLICE)) == 0 and tile >= HIST_A
    assert weights[6].shape[1:] == (D_MODEL, D_FF) and weights[7].shape[1:] == (D_FF, D_MODEL)
    tiles_per_seq = seq // tile
    n_tiles = batch * tiles_per_seq

    def front(s):
        i = jnp.minimum(s, n_tiles - 1)
        return i // tiles_per_seq, i % tiles_per_seq

    def back(s):
        i = jnp.maximum(s - 1, 0)
        return i // tiles_per_seq, i % tiles_per_seq

    def ahead(s):
        i = jnp.minimum(s + 1, n_tiles - 1)
        return i // tiles_per_seq, i % tiles_per_seq

    h_ahead = pl.BlockSpec((None, tile, D_MODEL), lambda s: (*ahead(s), 0))
    h_back = pl.BlockSpec((None, tile, D_MODEL), lambda s: (*back(s), 0))
    p_in = pl.BlockSpec((None, None, tile, D_PLE), lambda s: (layer, *back(s), 0))
    h_out = pl.BlockSpec((None, tile, D_MODEL), lambda s: (*back(s), 0))
    state_block = lambda rows, width: pl.BlockSpec((None, rows, width), lambda s: (front(s)[0], 0, 0))
    return pl.pallas_call(
        functools.partial(_prompt_kernel, tile=tile, n_tiles=n_tiles, tiles_per_seq=tiles_per_seq,
                          layer=layer, final=final),
        grid=(n_tiles + 1,),
        in_specs=[h_ahead, h_back, p_in] + [
            pl.BlockSpec(memory_space=pl.ANY) if i in _MLP_WEIGHT_POSITIONS else _layer_resident(w, layer)
            for i, w in enumerate(weights)],
        out_specs=[h_out, state_block(N_HIST_A, D_A),
                   state_block(POOL_STATE, D_B), state_block(N_HIST_C, D_C)],
        out_shape=[jax.ShapeDtypeStruct(h.shape, F32),
                   jax.ShapeDtypeStruct((batch, N_HIST_A, D_A), F32),
                   jax.ShapeDtypeStruct((batch, POOL_STATE, D_B), F32),
                   jax.ShapeDtypeStruct((batch, N_HIST_C, D_C), F32)],
        scratch_shapes=[pltpu.VMEM((tile, D_IN), F32),
                        pltpu.VMEM((D_A // LANES, HIST_A + tile, LANES), F32),
                        pltpu.VMEM((D_B // LANES, HIST_P + tile, LANES), F32),
                        pltpu.VMEM((D_C // LANES, HIST_C + tile, LANES), F32),
                        pltpu.VMEM((tile, D_B), BF16),
                        pltpu.VMEM((tile, D_MODEL), BF16),
                        pltpu.VMEM((tile, D_MODEL), BF16),
                        pltpu.VMEM((tile, D_MODEL), F32),
                        pltpu.VMEM((tile, D_MODEL), BF16),
                        pltpu.VMEM((2, tile, FF_SLICE), BF16),
                        pltpu.VMEM((D_MODEL, D_FF), BF16),
                        pltpu.VMEM((D_FF, D_MODEL), BF16),
                        pltpu.SemaphoreType.DMA((2,))],
        compiler_params=pltpu.CompilerParams(
            dimension_semantics=("arbitrary",),
            vmem_limit_bytes=VMEM_LIMIT_BYTES),
        name="prompt_layer",
    )(h, h, p_all, *weights)


def _sample_layers(h, p_all, st_a, st_p, st_c, weights):
    depth = p_all.shape[0]
    rows = h.shape[0]
    n_ff = D_FF // FF_SLICE
    vecs, w_in, caw, pool_bd, ccw, w_out, w_up, w_down, w_pg, w_pp = weights

    def per_layer(arr, **kwargs):
        zeros = (0,) * (arr.ndim - 1)
        return pl.BlockSpec((None,) + arr.shape[1:], lambda l, j: (l,) + zeros, **kwargs)

    once = dict(pipeline_mode=pl.Buffered(1))
    return pl.pallas_call(
        functools.partial(_sample_kernel, pos0=PAST_LEN),
        grid=(depth, n_ff),
        in_specs=[_resident(), per_layer(p_all), per_layer(st_a, **once), per_layer(st_p, **once),
                  per_layer(st_c, **once), per_layer(vecs), per_layer(w_in), per_layer(caw), per_layer(pool_bd),
                  per_layer(ccw), per_layer(w_out),
                  pl.BlockSpec((None, D_MODEL, FF_SLICE), lambda l, j: (l, 0, j)),
                  pl.BlockSpec((None, FF_SLICE, D_MODEL), lambda l, j: (l, j, 0)),
                  per_layer(w_pg), per_layer(w_pp)],
        out_specs=[_resident(), per_layer(st_a), per_layer(st_p), per_layer(st_c)],
        out_shape=[jax.ShapeDtypeStruct(h.shape, F32),
                   jax.ShapeDtypeStruct(st_a.shape, F32),
                   jax.ShapeDtypeStruct(st_p.shape, F32),
                   jax.ShapeDtypeStruct(st_c.shape, F32)],
        scratch_shapes=[pltpu.VMEM((rows, D_MODEL), BF16), pltpu.VMEM((rows, D_MODEL), F32),
                        pltpu.VMEM((rows, D_MODEL), F32), pltpu.VMEM((rows, D_MODEL), BF16)],
        compiler_params=pltpu.CompilerParams(
            dimension_semantics=("arbitrary", "arbitrary"),
            vmem_limit_bytes=VMEM_LIMIT_BYTES),
        name="sample_layers",
    )(h, p_all, st_a, st_p, st_c, *weights)


def _pool_block_diag(pool_w):
    depth, groups, c, _ = pool_w.shape
    eye = jnp.eye(groups, dtype=pool_w.dtype)
    return (eye[None, :, None, :, None] * pool_w[:, :, :, None, :]).reshape(depth, groups * c, groups * c)


def _history_major(x):
    return jnp.transpose(x, (0, 2, 1, 3))


def kernel(x_prompt, x_sample, state_conv_a, state_pool, state_conv_c, p_prompt, p_sample, norm_mix_g, w_in, conv_a_w, conv_a_b, ln_a_g, ln_a_b, pool_w, pool_scale, conv_c_w, w_out, norm_mlp_g, w_up, w_down, norm_ple_g, w_ple_gate, w_ple_proj, final_norm_g):
    depth = w_in.shape[0]
    dec_batch, dec_seq, _ = x_sample.shape
    assert dec_seq == 1
    hp = x_prompt
    hs = x_sample
    ps = p_sample.reshape(depth, dec_batch, D_PLE)
    sa, sp, sc = (_history_major(x) for x in (state_conv_a, state_pool, state_conv_c))
    vectors = dict(g_mix=norm_mix_g, g_mlp=norm_mlp_g, g_ple=norm_ple_g,
                   g_fin=jnp.broadcast_to(final_norm_g, (depth, D_MODEL)),
                   cab=conv_a_b, lng=ln_a_g, lnb=ln_a_b, pscale=pool_scale)
    vecs = jnp.concatenate([vectors[name] for name, _ in _VEC_SIZES], axis=-1).reshape(depth, 1, _VEC_WIDTH)
    weights = (vecs, w_in.astype(BF16), conv_a_w, _pool_block_diag(pool_w).astype(BF16), conv_c_w,
               w_out.astype(BF16), w_up.astype(BF16), w_down.astype(BF16),
               w_ple_gate.astype(BF16), w_ple_proj.astype(BF16))
    prompt_states = []
    for i in range(depth):
        hp, *states = _prompt_layer(hp, p_prompt, i, weights, final=i == depth - 1)
        prompt_states.append(states)
    hs, *sample_states = _sample_layers(hs, ps, sa, sp, sc, weights)
    return (hp, hs, *(jnp.stack(x) for x in zip(*prompt_states)),
            *(_history_major(x) for x in sample_states))
```

```python
import functools

import jax
import jax.numpy as jnp
from jax import lax
from jax.experimental import pallas as pl
from jax.experimental.pallas import tpu as pltpu

D_MODEL = 1024
D_A = 384
D_B = 256
D_C = 384
D_IN = 2 * D_A + D_B + 3 * D_C
D_FF = 4096
D_PLE = 256
POOL_WINDOWS = (2, 4, 8, 16)
POOL_GROUP_DIM = D_B // len(POOL_WINDOWS)
POOL_STATE = 15
CONV_A_WIDTH = 31
CONV_C_WIDTH = 3
N_HIST_A = CONV_A_WIDTH - 1
N_HIST_C = CONV_C_WIDTH - 1
PAST_LEN = 16384
EPS = 1e-6

_O_AV = 0
_O_AG = D_A
_O_UB = 2 * D_A
_O_CB = _O_UB + D_B
_O_CC = _O_CB + D_C
_O_CX = _O_CC + D_C
_M_A = 0
_M_B = D_A
_M_C = D_A + D_B

LANES = 128
HIST_A = 32
HIST_P = 16
HIST_C = 8
ROW_CHUNK = 64
SAMPLE_ROW_CHUNK = 32
FF_SLICE = 1024
SEQ_TILE = 512
VMEM_LIMIT_BYTES = 62 * 1024 * 1024

BF16 = jnp.bfloat16
F32 = jnp.float32


_VEC_SIZES = (("g_mix", D_MODEL), ("g_mlp", D_MODEL), ("g_ple", D_MODEL), ("g_fin", D_MODEL),
              ("cab", D_A), ("lng", D_A), ("lnb", D_A), ("pscale", D_B))
_VEC_LAYOUT = {}
_VEC_WIDTH = 0
for _name, _size in _VEC_SIZES:
    _VEC_LAYOUT[_name] = (_VEC_WIDTH, _size)
    _VEC_WIDTH += _size


class _Cols:
    def __init__(self, ref, start, size):
        self.ref, self.start, self.size = ref, start, size

    def __getitem__(self, idx):
        if idx is Ellipsis:
            return self.ref[:, self.start:self.start + self.size]
        rows, cols = idx
        lo = cols.start or 0
        hi = self.size if cols.stop is None else cols.stop
        return self.ref[rows, self.start + lo:self.start + hi]


def _vector_params(vecs):
    return tuple(_Cols(vecs, *_VEC_LAYOUT[name]) for name, _ in _VEC_SIZES)


def _rms_norm(x, g):
    ms = jnp.mean(x * x, axis=-1, keepdims=True)
    return x * lax.rsqrt(ms + EPS) * g


def _dot(a, b):
    return jnp.dot(a, b, preferred_element_type=F32)


def _layer_norm_silu(ya, g, b):
    mu = jnp.mean(ya, axis=-1, keepdims=True)
    xc = ya - mu
    var = jnp.mean(xc * xc, axis=-1, keepdims=True)
    y = xc * lax.rsqrt(var + EPS) * g + b
    return y * jax.nn.sigmoid(y)


def _pool_select(s_small, s_big, w_small, w_big, pos1):
    lane = lax.broadcasted_iota(jnp.int32, s_small.shape, 1)
    cnt_small = jnp.minimum(pos1, w_small).astype(F32)
    cnt_big = jnp.minimum(pos1, w_big).astype(F32)
    return jnp.where(lane < POOL_GROUP_DIM, s_small / cnt_small, s_big / cnt_big)


def _lane_cols(offset, c):
    return slice(offset + c * LANES, offset + (c + 1) * LANES)


def _gated_embedding(h2, p, g_ple, w_pg, w_pp, g_fin, final):
    r3 = _rms_norm(h2, g_ple[...]).astype(BF16)
    gate = jax.nn.sigmoid(_dot(r3, w_pg[...]))
    h3 = h2 + gate * _dot(p.astype(BF16), w_pp[...])
    if final:
        h3 = _rms_norm(h3, g_fin[...])
    return h3


def _prompt_kernel(hn_ref, hb_ref, p_ref, vecs, w_in, caw, pool_bd, ccw, w_out, w_up_hbm, w_down_hbm, w_pg, w_pp,
                   *rest, tile, n_tiles, tiles_per_seq, layer, final):
    (out_ref, na_ref, np_ref, nc_ref,
     zbuf, gbuf, ubuf, vbuf, poolbuf, mixbuf, nbuf, h2buf, mbuf, actbuf, w_up, w_down, w_sem) = rest
    g_mix, g_mlp, g_ple, g_fin, cab, lng, lnb, pscale = _vector_params(vecs)
    s = pl.program_id(0)
    t = s % tiles_per_seq

    def mlp_weight_copies():
        return (pltpu.make_async_copy(w_up_hbm.at[layer], w_up, w_sem.at[0]),
                pltpu.make_async_copy(w_down_hbm.at[layer], w_down, w_sem.at[1]))

    @pl.when(s == 0)
    def _():
        for copy in mlp_weight_copies():
            copy.start()

    @pl.when(s == 1)
    def _():
        for copy in mlp_weight_copies():
            copy.wait()

    def shifted(buf, c, start):
        return buf[c, pl.ds(start, ROW_CHUNK, stride=1), :]

    def mixers(r0):
        rows = pl.ds(r0, ROW_CHUNK)
        for c in range(D_A // LANES):
            glu = zbuf[rows, _lane_cols(_O_AV, c)] * jax.nn.sigmoid(zbuf[rows, _lane_cols(_O_AG, c)])
            gbuf[c, pl.ds(r0 + HIST_A, ROW_CHUNK), :] = glu
        for c in range(D_B // LANES):
            ubuf[c, pl.ds(r0 + HIST_P, ROW_CHUNK), :] = zbuf[rows, _lane_cols(_O_UB, c)]
        for c in range(D_C // LANES):
            vbuf[c, pl.ds(r0 + HIST_C, ROW_CHUNK), :] = (
                zbuf[rows, _lane_cols(_O_CC, c)] * zbuf[rows, _lane_cols(_O_CX, c)])

        conv_a = []
        for c in range(D_A // LANES):
            acc = jnp.broadcast_to(cab[:, _lane_cols(0, c)], (ROW_CHUNK, LANES))
            for k in range(CONV_A_WIDTH):
                acc = acc + caw[k:k + 1, _lane_cols(0, c)] * shifted(gbuf, c, r0 + HIST_A - N_HIST_A + k)
            conv_a.append(acc)
        out_a = _layer_norm_silu(jnp.concatenate(conv_a, axis=-1), lng[...], lnb[...])
        mixbuf[rows, _M_A:_M_A + D_A] = out_a.astype(BF16)

        pos1 = (t * tile + r0 + 1) + lax.broadcasted_iota(jnp.int32, (ROW_CHUNK, LANES), 0)
        u_lo = [shifted(ubuf, 0, r0 + HIST_P - j) for j in range(4)]
        s2 = u_lo[0] + u_lo[1]
        s4 = s2 + (u_lo[2] + u_lo[3])
        pooled_lo = _pool_select(s2, s4, 2, 4, pos1) - u_lo[0]
        u_hi = [shifted(ubuf, 1, r0 + HIST_P - j) for j in range(16)]
        s8 = u_hi[0]
        for j in range(1, 8):
            s8 = s8 + u_hi[j]
        s16 = s8
        for j in range(8, 16):
            s16 = s16 + u_hi[j]
        pooled_hi = _pool_select(s8, s16, 8, 16, pos1) - u_hi[0]
        poolbuf[rows, 0:LANES] = pooled_lo.astype(BF16)
        poolbuf[rows, LANES:2 * LANES] = pooled_hi.astype(BF16)

        for c in range(D_C // LANES):
            conv_c = ccw[2:3, _lane_cols(0, c)] * shifted(vbuf, c, r0 + HIST_C)
            conv_c = conv_c + ccw[1:2, _lane_cols(0, c)] * shifted(vbuf, c, r0 + HIST_C - 1)
            conv_c = conv_c + ccw[0:1, _lane_cols(0, c)] * shifted(vbuf, c, r0 + HIST_C - 2)
            out_c = zbuf[rows, _lane_cols(_O_CB, c)] * conv_c
            mixbuf[rows, _lane_cols(_M_C, c)] = out_c.astype(BF16)

    n_ff = D_FF // FF_SLICE
    chunks_per_slice = tile // ROW_CHUNK // n_ff

    def mixer_blocks(j):
        for q in range(chunks_per_slice):
            mixers((j * chunks_per_slice + q) * ROW_CHUNK)

    def up_act(j):
        up = _dot(mbuf[...], w_up[:, j * FF_SLICE:(j + 1) * FF_SLICE])
        actbuf[j % 2] = jnp.square(jnp.maximum(up, 0.0)).astype(BF16)

    def step(front, back):
        if front:
            @pl.when(t == 0)
            def _():
                gbuf[:, 0:HIST_A, :] = jnp.zeros((D_A // LANES, HIST_A, LANES), F32)
                ubuf[:, 0:HIST_P, :] = jnp.zeros((D_B // LANES, HIST_P, LANES), F32)
                vbuf[:, 0:HIST_C, :] = jnp.zeros((D_C // LANES, HIST_C, LANES), F32)

        if front and not back:
            nbuf[...] = _rms_norm(hb_ref[...], g_mix[...]).astype(BF16)
        if back:
            h1 = hb_ref[...] + _dot(mixbuf[...], w_out[...])
        if front:
            zbuf[...] = _dot(nbuf[...], w_in[...])
            nbuf[...] = _rms_norm(hn_ref[...], g_mix[...]).astype(BF16)
        if back:
            h2buf[...] = h1
            mbuf[...] = _rms_norm(h1, g_mlp[...]).astype(BF16)
            up_act(0)

        for j in range(n_ff):
            if front:
                mixer_blocks(j)
            if back:
                down = _dot(actbuf[j % 2], w_down[j * FF_SLICE:(j + 1) * FF_SLICE, :])
                if j + 1 < n_ff:
                    up_act(j + 1)
                h2buf[...] += down

        if front:
            out_b = _dot(poolbuf[...], pool_bd[...]) * pscale[...]
            mixbuf[:, _M_B:_M_B + D_B] = out_b.astype(BF16)
            gbuf[:, 0:HIST_A, :] = gbuf[:, tile:tile + HIST_A, :]
            ubuf[:, 0:HIST_P, :] = ubuf[:, tile:tile + HIST_P, :]
            vbuf[:, 0:HIST_C, :] = vbuf[:, tile:tile + HIST_C, :]
        if back:
            out_ref[...] = _gated_embedding(h2buf[...], p_ref[...], g_ple, w_pg, w_pp, g_fin, final)

        if front:
            @pl.when(t == tiles_per_seq - 1)
            def _():
                for c in range(D_A // LANES):
                    na_ref[:, _lane_cols(0, c)] = gbuf[c, HIST_A + tile - N_HIST_A:HIST_A + tile, :]
                for c in range(D_B // LANES):
                    np_ref[:, _lane_cols(0, c)] = ubuf[c, HIST_P + tile - POOL_STATE:HIST_P + tile, :]
                for c in range(D_C // LANES):
                    nc_ref[:, _lane_cols(0, c)] = vbuf[c, HIST_C + tile - N_HIST_C:HIST_C + tile, :]

    pl.when(s == 0)(lambda: step(True, False))
    pl.when((s > 0) & (s < n_tiles))(lambda: step(True, True))
    pl.when(s == n_tiles)(lambda: step(False, True))


def _sample_kernel(h_ref, p_ref, sa_ref, sp_ref, sc_ref, vecs, w_in, caw, pool_bd, ccw, w_out, w_up, w_down, w_pg, w_pp,
                   out_ref, na_ref, np_ref, nc_ref, mixbuf, hbuf, h2buf, mbuf, *, pos0):
    g_mix, g_mlp, g_ple, g_fin, cab, lng, lnb, pscale = _vector_params(vecs)
    layer = pl.program_id(0)
    j = pl.program_id(1)

    @pl.when(j == 0)
    def _():
        @pl.when(layer == 0)
        def _():
            hbuf[...] = h_ref[:, 0, :]

        h = hbuf[...]
        rows = h.shape[0]
        z = _dot(_rms_norm(h, g_mix[...]).astype(BF16), w_in[...])
        u = z[:, _O_UB:_O_UB + D_B]
        v = z[:, _O_CC:_O_CC + D_C] * z[:, _O_CX:_O_CX + D_C]
        glu = z[:, _O_AV:_O_AV + D_A] * jax.nn.sigmoid(z[:, _O_AG:_O_AG + D_A])

        for r0 in range(0, rows, SAMPLE_ROW_CHUNK):
            blk = slice(r0, r0 + SAMPLE_ROW_CHUNK)
            acc = cab[...] + caw[N_HIST_A:N_HIST_A + 1, :] * glu[blk]
            for k in range(N_HIST_A):
                acc = acc + caw[k:k + 1, :] * sa_ref[k, blk, :]
            mixbuf[blk, _M_A:_M_A + D_A] = _layer_norm_silu(acc, lng[...], lnb[...]).astype(BF16)
        na_ref[0:N_HIST_A - 1] = sa_ref[1:N_HIST_A]
        na_ref[N_HIST_A - 1] = glu

        pos1 = jnp.full((rows, LANES), pos0 + 1, jnp.int32)
        u_lo = u[:, 0:LANES]
        s2 = u_lo + sp_ref[POOL_STATE - 1, :, 0:LANES]
        s4 = s2 + (sp_ref[POOL_STATE - 2, :, 0:LANES] + sp_ref[POOL_STATE - 3, :, 0:LANES])
        pooled_lo = _pool_select(s2, s4, 2, 4, pos1) - u_lo
        u_hi = u[:, LANES:2 * LANES]
        s8 = u_hi
        for i in range(1, 8):
            s8 = s8 + sp_ref[POOL_STATE - i, :, LANES:2 * LANES]
        s16 = s8
        for i in range(8, 16):
            s16 = s16 + sp_ref[POOL_STATE - i, :, LANES:2 * LANES]
        pooled_hi = _pool_select(s8, s16, 8, 16, pos1) - u_hi
        pooled = jnp.concatenate([pooled_lo, pooled_hi], axis=-1).astype(BF16)
        mixbuf[:, _M_B:_M_B + D_B] = (_dot(pooled, pool_bd[...]) * pscale[...]).astype(BF16)
        np_ref[0:POOL_STATE - 1] = sp_ref[1:POOL_STATE]
        np_ref[POOL_STATE - 1] = u

        conv_c = ccw[2:3, :] * v + ccw[1:2, :] * sc_ref[:, 1, :] + ccw[0:1, :] * sc_ref[:, 0, :]
        mixbuf[:, _M_C:_M_C + D_C] = (z[:, _O_CB:_O_CB + D_C] * conv_c).astype(BF16)
        nc_ref[:, 0, :] = sc_ref[:, 1, :]
        nc_ref[:, 1, :] = v

        h1 = h + _dot(mixbuf[...], w_out[...])
        h2buf[...] = h1
        mbuf[...] = _rms_norm(h1, g_mlp[...]).astype(BF16)

    act = jnp.square(jnp.maximum(_dot(mbuf[...], w_up[...]), 0.0)).astype(BF16)
    h2buf[...] += _dot(act, w_down[...])

    @pl.when(j == pl.num_programs(1) - 1)
    def _():
        h3 = _gated_embedding(h2buf[...], p_ref[:, 0, :], g_ple, w_pg, w_pp, g_fin, False)
        hbuf[...] = h3

        @pl.when(layer == pl.num_programs(0) - 1)
        def _():
            out_ref[:, 0, :] = _rms_norm(h3, g_fin[...])


_MLP_WEIGHT_POSITIONS = (6, 7)


def _resident():
    return pl.BlockSpec(memory_space=pltpu.VMEM)


def _layer_resident(arr, layer):
    zeros = (0,) * (arr.ndim - 1)
    return pl.BlockSpec((None,) + arr.shape[1:], lambda s: (layer,) + zeros, pipeline_mode=pl.Buffered(1))


def _prompt_layer(h, p_all, layer, weights, *, final):
    batch, seq, _ = h.shape
    tile = min(SEQ_TILE, seq)
    assert seq % tile == 0 and tile % (ROW_CHUNK * (D_FF // FF_SLICE)) == 0 and tile >= HIST_A
    assert weights[6].shape[1:] == (D_MODEL, D_FF) and weights[7].shape[1:] == (D_FF, D_MODEL)
    tiles_per_seq = seq // tile
    n_tiles = batch * tiles_per_seq

    def front(s):
        i = jnp.minimum(s, n_tiles - 1)
        return i // tiles_per_seq, i % tiles_per_seq

    def back(s):
        i = jnp.maximum(s - 1, 0)
        return i // tiles_per_seq, i % tiles_per_seq

    def ahead(s):
        i = jnp.minimum(s + 1, n_tiles - 1)
        return i // tiles_per_seq, i % tiles_per_seq

    h_ahead = pl.BlockSpec((None, tile, D_MODEL), lambda s: (*ahead(s), 0))
    h_back = pl.BlockSpec((None, tile, D_MODEL), lambda s: (*back(s), 0))
    p_in = pl.BlockSpec((None, None, tile, D_PLE), lambda s: (layer, *back(s), 0))
    h_out = pl.BlockSpec((None, tile, D_MODEL), lambda s: (*back(s), 0))
    state_block = lambda rows, width: pl.BlockSpec((None, rows, width), lambda s: (front(s)[0], 0, 0))
    return pl.pallas_call(
        functools.partial(_prompt_kernel, tile=tile, n_tiles=n_tiles, tiles_per_seq=tiles_per_seq,
                          layer=layer, final=final),
        grid=(n_tiles + 1,),
        in_specs=[h_ahead, h_back, p_in] + [
            pl.BlockSpec(memory_space=pl.ANY) if i in _MLP_WEIGHT_POSITIONS else _layer_resident(w, layer)
            for i, w in enumerate(weights)],
        out_specs=[h_out, state_block(N_HIST_A, D_A),
                   state_block(POOL_STATE, D_B), state_block(N_HIST_C, D_C)],
        out_shape=[jax.ShapeDtypeStruct(h.shape, F32),
                   jax.ShapeDtypeStruct((batch, N_HIST_A, D_A), F32),
                   jax.ShapeDtypeStruct((batch, POOL_STATE, D_B), F32),
                   jax.ShapeDtypeStruct((batch, N_HIST_C, D_C), F32)],
        scratch_shapes=[pltpu.VMEM((tile, D_IN), F32),
                        pltpu.VMEM((D_A // LANES, HIST_A + tile, LANES), F32),
                        pltpu.VMEM((D_B // LANES, HIST_P + tile, LANES), F32),
                        pltpu.VMEM((D_C // LANES, HIST_C + tile, LANES), F32),
                        pltpu.VMEM((tile, D_B), BF16),
                        pltpu.VMEM((tile, D_MODEL), BF16),
                        pltpu.VMEM((tile, D_MODEL), BF16),
                        pltpu.VMEM((tile, D_MODEL), F32),
                        pltpu.VMEM((tile, D_MODEL), BF16),
                        pltpu.VMEM((2, tile, FF_SLICE), BF16),
                        pltpu.VMEM((D_MODEL, D_FF), BF16),
                        pltpu.VMEM((D_FF, D_MODEL), BF16),
                        pltpu.SemaphoreType.DMA((2,))],
        compiler_params=pltpu.CompilerParams(
            dimension_semantics=("arbitrary",),
            vmem_limit_bytes=VMEM_LIMIT_BYTES),
        name="prompt_layer",
    )(h, h, p_all, *weights)


def _sample_layers(h, p_all, st_a, st_p, st_c, weights):
    depth = p_all.shape[0]
    rows = h.shape[0]
    n_ff = D_FF // FF_SLICE
    vecs, w_in, caw, pool_bd, ccw, w_out, w_up, w_down, w_pg, w_pp = weights

    def per_layer(arr, **kwargs):
        zeros = (0,) * (arr.ndim - 1)
        return pl.BlockSpec((None,) + arr.shape[1:], lambda l, j: (l,) + zeros, **kwargs)

    once = dict(pipeline_mode=pl.Buffered(1))
    return pl.pallas_call(
        functools.partial(_sample_kernel, pos0=PAST_LEN),
        grid=(depth, n_ff),
        in_specs=[_resident(), per_layer(p_all), per_layer(st_a, **once), per_layer(st_p, **once),
                  per_layer(st_c, **once), per_layer(vecs), per_layer(w_in), per_layer(caw), per_layer(pool_bd),
                  per_layer(ccw), per_layer(w_out),
                  pl.BlockSpec((None, D_MODEL, FF_SLICE), lambda l, j: (l, 0, j)),
                  pl.BlockSpec((None, FF_SLICE, D_MODEL), lambda l, j: (l, j, 0)),
                  per_layer(w_pg), per_layer(w_pp)],
        out_specs=[_resident(), per_layer(st_a), per_layer(st_p), per_layer(st_c)],
        out_shape=[jax.ShapeDtypeStruct(h.shape, F32),
                   jax.ShapeDtypeStruct(st_a.shape, F32),
                   jax.ShapeDtypeStruct(st_p.shape, F32),
                   jax.ShapeDtypeStruct(st_c.shape, F32)],
        scratch_shapes=[pltpu.VMEM((rows, D_MODEL), BF16), pltpu.VMEM((rows, D_MODEL), F32),
                        pltpu.VMEM((rows, D_MODEL), F32), pltpu.VMEM((rows, D_MODEL), BF16)],
        compiler_params=pltpu.CompilerParams(
            dimension_semantics=("arbitrary", "arbitrary"),
            vmem_limit_bytes=VMEM_LIMIT_BYTES),
        name="sample_layers",
    )(h, p_all, st_a, st_p, st_c, *weights)


def _pool_block_diag(pool_w):
    depth, groups, c, _ = pool_w.shape
    eye = jnp.eye(groups, dtype=pool_w.dtype)
    return (eye[None, :, None, :, None] * pool_w[:, :, :, None, :]).reshape(depth, groups * c, groups * c)


def _history_major(x):
    return jnp.transpose(x, (0, 2, 1, 3))


def kernel(x_prompt, x_sample, state_conv_a, state_pool, state_conv_c, p_prompt, p_sample, norm_mix_g, w_in, conv_a_w, conv_a_b, ln_a_g, ln_a_b, pool_w, pool_scale, conv_c_w, w_out, norm_mlp_g, w_up, w_down, norm_ple_g, w_ple_gate, w_ple_proj, final_norm_g):
    depth = w_in.shape[0]
    dec_batch, dec_seq, _ = x_sample.shape
    assert dec_seq == 1
    hp = x_prompt
    hs = x_sample
    sa, sp = _history_major(state_conv_a), _history_major(state_pool)
    vectors = dict(g_mix=norm_mix_g, g_mlp=norm_mlp_g, g_ple=norm_ple_g,
                   g_fin=jnp.broadcast_to(final_norm_g, (depth, D_MODEL)),
                   cab=conv_a_b, lng=ln_a_g, lnb=ln_a_b, pscale=pool_scale)
    vecs = jnp.concatenate([vectors[name] for name, _ in _VEC_SIZES], axis=-1).reshape(depth, 1, _VEC_WIDTH)
    weights = (vecs, w_in.astype(BF16), conv_a_w, _pool_block_diag(pool_w).astype(BF16), conv_c_w,
               w_out.astype(BF16), w_up.astype(BF16), w_down.astype(BF16),
               w_ple_gate.astype(BF16), w_ple_proj.astype(BF16))
    prompt_states = []
    for i in range(depth):
        hp, *states = _prompt_layer(hp, p_prompt, i, weights, final=i == depth - 1)
        prompt_states.append(states)
    hs, new_sa, new_sp, new_sc = _sample_layers(hs, p_sample, sa, sp, state_conv_c, weights)
    return (hp, hs, *(jnp.stack(x) for x in zip(*prompt_states)),
            _history_major(new_sa), _history_major(new_sp), new_sc)
```

```python
import functools

import jax
import jax.numpy as jnp
from jax import lax
from jax.experimental import pallas as pl
from jax.experimental.pallas import tpu as pltpu

D_MODEL = 1024
D_A = 384
D_B = 256
D_C = 384
D_IN = 2 * D_A + D_B + 3 * D_C
D_FF = 4096
D_PLE = 256
POOL_WINDOWS = (2, 4, 8, 16)
POOL_GROUP_DIM = D_B // len(POOL_WINDOWS)
POOL_STATE = 15
CONV_A_WIDTH = 31
CONV_C_WIDTH = 3
N_HIST_A = CONV_A_WIDTH - 1
N_HIST_C = CONV_C_WIDTH - 1
PAST_LEN = 16384
EPS = 1e-6

_O_AV = 0
_O_AG = D_A
_O_UB = 2 * D_A
_O_CB = _O_UB + D_B
_O_CC = _O_CB + D_C
_O_CX = _O_CC + D_C
_M_A = 0
_M_B = D_A
_M_C = D_A + D_B

LANES = 128
HIST_A = 32
HIST_P = 16
HIST_C = 8
ROW_CHUNK = 64
SAMPLE_ROW_CHUNK = 32
FF_SLICE = 1024
SEQ_TILE = 512
VMEM_LIMIT_BYTES = 62 * 1024 * 1024

BF16 = jnp.bfloat16
F32 = jnp.float32


_VEC_SIZES = (("g_mix", D_MODEL), ("g_mlp", D_MODEL), ("g_ple", D_MODEL), ("g_fin", D_MODEL),
              ("cab", D_A), ("lng", D_A), ("lnb", D_A), ("pscale", D_B))
_VEC_LAYOUT = {}
_VEC_WIDTH = 0
for _name, _size in _VEC_SIZES:
    _VEC_LAYOUT[_name] = (_VEC_WIDTH, _size)
    _VEC_WIDTH += _size


class _Cols:
    def __init__(self, ref, start, size):
        self.ref, self.start, self.size = ref, start, size

    def __getitem__(self, idx):
        if idx is Ellipsis:
            return self.ref[:, self.start:self.start + self.size]
        rows, cols = idx
        lo = cols.start or 0
        hi = self.size if cols.stop is None else cols.stop
        return self.ref[rows, self.start + lo:self.start + hi]


def _vector_params(vecs):
    return tuple(_Cols(vecs, *_VEC_LAYOUT[name]) for name, _ in _VEC_SIZES)


def _rms_norm(x, g):
    ms = jnp.mean(x * x, axis=-1, keepdims=True)
    return x * lax.rsqrt(ms + EPS) * g


def _dot(a, b):
    return jnp.dot(a, b, preferred_element_type=F32)


def _layer_norm_silu(ya, g, b):
    mu = jnp.mean(ya, axis=-1, keepdims=True)
    xc = ya - mu
    var = jnp.mean(xc * xc, axis=-1, keepdims=True)
    y = xc * lax.rsqrt(var + EPS) * g + b
    return y * jax.nn.sigmoid(y)


def _pool_select(s_small, s_big, w_small, w_big, pos1):
    lane = lax.broadcasted_iota(jnp.int32, s_small.shape, 1)
    cnt_small = jnp.minimum(pos1, w_small).astype(F32)
    cnt_big = jnp.minimum(pos1, w_big).astype(F32)
    return jnp.where(lane < POOL_GROUP_DIM, s_small / cnt_small, s_big / cnt_big)


def _lane_cols(offset, c):
    return slice(offset + c * LANES, offset + (c + 1) * LANES)


def _gated_embedding(h2, p, g_ple, w_pg, w_pp, g_fin, final):
    r3 = _rms_norm(h2, g_ple[...]).astype(BF16)
    gate = jax.nn.sigmoid(_dot(r3, w_pg[...]))
    h3 = h2 + gate * _dot(p.astype(BF16), w_pp[...])
    if final:
        h3 = _rms_norm(h3, g_fin[...])
    return h3


def _prompt_kernel(hn_ref, hb_ref, p_ref, vecs, w_in, caw, pool_bd, ccw, w_out, w_up_hbm, w_down_hbm, w_pg, w_pp,
                   *rest, tile, n_tiles, tiles_per_seq, layer, final):
    prev_states, rest = rest[:3 if layer else 0], rest[3 if layer else 0:]
    (out_ref, na_ref, np_ref, nc_ref,
     zbuf, gbuf, ubuf, vbuf, poolbuf, mixbuf, nbuf, h2buf, mbuf, actbuf, w_up, w_down, w_sem) = rest
    g_mix, g_mlp, g_ple, g_fin, cab, lng, lnb, pscale = _vector_params(vecs)
    s = pl.program_id(0)
    t = s % tiles_per_seq

    def mlp_weight_copies():
        return (pltpu.make_async_copy(w_up_hbm.at[layer], w_up, w_sem.at[0]),
                pltpu.make_async_copy(w_down_hbm.at[layer], w_down, w_sem.at[1]))

    @pl.when(s == 0)
    def _():
        for copy in mlp_weight_copies():
            copy.start()
        for prev, new in zip(prev_states, (na_ref, np_ref, nc_ref)):
            new[0:layer] = prev[...]

    @pl.when(s == 1)
    def _():
        for copy in mlp_weight_copies():
            copy.wait()

    def shifted(buf, c, start):
        return buf[c, pl.ds(start, ROW_CHUNK, stride=1), :]

    def mixers(r0):
        rows = pl.ds(r0, ROW_CHUNK)
        for c in range(D_A // LANES):
            glu = zbuf[rows, _lane_cols(_O_AV, c)] * jax.nn.sigmoid(zbuf[rows, _lane_cols(_O_AG, c)])
            gbuf[c, pl.ds(r0 + HIST_A, ROW_CHUNK), :] = glu
        for c in range(D_B // LANES):
            ubuf[c, pl.ds(r0 + HIST_P, ROW_CHUNK), :] = zbuf[rows, _lane_cols(_O_UB, c)]
        for c in range(D_C // LANES):
            vbuf[c, pl.ds(r0 + HIST_C, ROW_CHUNK), :] = (
                zbuf[rows, _lane_cols(_O_CC, c)] * zbuf[rows, _lane_cols(_O_CX, c)])

        conv_a = []
        for c in range(D_A // LANES):
            acc = jnp.broadcast_to(cab[:, _lane_cols(0, c)], (ROW_CHUNK, LANES))
            for k in range(CONV_A_WIDTH):
                acc = acc + caw[k:k + 1, _lane_cols(0, c)] * shifted(gbuf, c, r0 + HIST_A - N_HIST_A + k)
            conv_a.append(acc)
        out_a = _layer_norm_silu(jnp.concatenate(conv_a, axis=-1), lng[...], lnb[...])
        mixbuf[rows, _M_A:_M_A + D_A] = out_a.astype(BF16)

        pos1 = (t * tile + r0 + 1) + lax.broadcasted_iota(jnp.int32, (ROW_CHUNK, LANES), 0)
        u_lo = [shifted(ubuf, 0, r0 + HIST_P - j) for j in range(4)]
        s2 = u_lo[0] + u_lo[1]
        s4 = s2 + (u_lo[2] + u_lo[3])
        pooled_lo = _pool_select(s2, s4, 2, 4, pos1) - u_lo[0]
        u_hi = [shifted(ubuf, 1, r0 + HIST_P - j) for j in range(16)]
        s8 = u_hi[0]
        for j in range(1, 8):
            s8 = s8 + u_hi[j]
        s16 = s8
        for j in range(8, 16):
            s16 = s16 + u_hi[j]
        pooled_hi = _pool_select(s8, s16, 8, 16, pos1) - u_hi[0]
        poolbuf[rows, 0:LANES] = pooled_lo.astype(BF16)
        poolbuf[rows, LANES:2 * LANES] = pooled_hi.astype(BF16)

        for c in range(D_C // LANES):
            conv_c = ccw[2:3, _lane_cols(0, c)] * shifted(vbuf, c, r0 + HIST_C)
            conv_c = conv_c + ccw[1:2, _lane_cols(0, c)] * shifted(vbuf, c, r0 + HIST_C - 1)
            conv_c = conv_c + ccw[0:1, _lane_cols(0, c)] * shifted(vbuf, c, r0 + HIST_C - 2)
            out_c = zbuf[rows, _lane_cols(_O_CB, c)] * conv_c
            mixbuf[rows, _lane_cols(_M_C, c)] = out_c.astype(BF16)

    n_ff = D_FF // FF_SLICE
    chunks_per_slice = tile // ROW_CHUNK // n_ff

    def mixer_blocks(j):
        for q in range(chunks_per_slice):
            mixers((j * chunks_per_slice + q) * ROW_CHUNK)

    def up_act(j):
        up = _dot(mbuf[...], w_up[:, j * FF_SLICE:(j + 1) * FF_SLICE])
        actbuf[j % 2] = jnp.square(jnp.maximum(up, 0.0)).astype(BF16)

    def step(front, back):
        if front:
            @pl.when(t == 0)
            def _():
                gbuf[:, 0:HIST_A, :] = jnp.zeros((D_A // LANES, HIST_A, LANES), F32)
                ubuf[:, 0:HIST_P, :] = jnp.zeros((D_B // LANES, HIST_P, LANES), F32)
                vbuf[:, 0:HIST_C, :] = jnp.zeros((D_C // LANES, HIST_C, LANES), F32)

        if front and not back:
            nbuf[...] = _rms_norm(hb_ref[...], g_mix[...]).astype(BF16)
        if back:
            h1 = hb_ref[...] + _dot(mixbuf[...], w_out[...])
        if front:
            zbuf[...] = _dot(nbuf[...], w_in[...])
            nbuf[...] = _rms_norm(hn_ref[...], g_mix[...]).astype(BF16)
        if back:
            h2buf[...] = h1
            mbuf[...] = _rms_norm(h1, g_mlp[...]).astype(BF16)
            up_act(0)

        for j in range(n_ff):
            if front:
                mixer_blocks(j)
            if back:
                down = _dot(actbuf[j % 2], w_down[j * FF_SLICE:(j + 1) * FF_SLICE, :])
                if j + 1 < n_ff:
                    up_act(j + 1)
                h2buf[...] += down

        if front:
            out_b = _dot(poolbuf[...], pool_bd[...]) * pscale[...]
            mixbuf[:, _M_B:_M_B + D_B] = out_b.astype(BF16)
            gbuf[:, 0:HIST_A, :] = gbuf[:, tile:tile + HIST_A, :]
            ubuf[:, 0:HIST_P, :] = ubuf[:, tile:tile + HIST_P, :]
            vbuf[:, 0:HIST_C, :] = vbuf[:, tile:tile + HIST_C, :]
        if back:
            out_ref[...] = _gated_embedding(h2buf[...], p_ref[...], g_ple, w_pg, w_pp, g_fin, final)

        if front:
            @pl.when(t == tiles_per_seq - 1)
            def _():
                for c in range(D_C // LANES):
                    nc_ref[layer, s // tiles_per_seq, :, _lane_cols(0, c)] = (
                        vbuf[c, HIST_C + tile - N_HIST_C:HIST_C + tile, :])

                for b in range(n_tiles // tiles_per_seq):
                    @pl.when(s == (b + 1) * tiles_per_seq - 1)
                    def _(b=b):
                        for c in range(D_A // LANES):
                            na_ref[layer, :, b, _lane_cols(0, c)] = gbuf[c, HIST_A + tile - N_HIST_A:HIST_A + tile, :]
                        for c in range(D_B // LANES):
                            np_ref[layer, :, b, _lane_cols(0, c)] = ubuf[c, HIST_P + tile - POOL_STATE:HIST_P + tile, :]

    pl.when(s == 0)(lambda: step(True, False))
    pl.when((s > 0) & (s < n_tiles))(lambda: step(True, True))
    pl.when(s == n_tiles)(lambda: step(False, True))


def _sample_kernel(h_ref, p_ref, sa_ref, sp_ref, sc_ref, vecs, w_in, caw, pool_bd, ccw, w_out, w_up, w_down, w_pg, w_pp,
                   out_ref, na_ref, np_ref, nc_ref, mixbuf, hbuf, h2buf, mbuf, *, pos0):
    g_mix, g_mlp, g_ple, g_fin, cab, lng, lnb, pscale = _vector_params(vecs)
    layer = pl.program_id(0)
    j = pl.program_id(1)

    @pl.when(j == 0)
    def _():
        @pl.when(layer == 0)
        def _():
            hbuf[...] = h_ref[:, 0, :]

        h = hbuf[...]
        rows = h.shape[0]
        z = _dot(_rms_norm(h, g_mix[...]).astype(BF16), w_in[...])
        u = z[:, _O_UB:_O_UB + D_B]
        v = z[:, _O_CC:_O_CC + D_C] * z[:, _O_CX:_O_CX + D_C]
        glu = z[:, _O_AV:_O_AV + D_A] * jax.nn.sigmoid(z[:, _O_AG:_O_AG + D_A])

        for r0 in range(0, rows, SAMPLE_ROW_CHUNK):
            blk = slice(r0, r0 + SAMPLE_ROW_CHUNK)
            acc = cab[...] + caw[N_HIST_A:N_HIST_A + 1, :] * glu[blk]
            for k in range(N_HIST_A):
                acc = acc + caw[k:k + 1, :] * sa_ref[k, blk, :]
            mixbuf[blk, _M_A:_M_A + D_A] = _layer_norm_silu(acc, lng[...], lnb[...]).astype(BF16)
        na_ref[0:N_HIST_A - 1] = sa_ref[1:N_HIST_A]
        na_ref[N_HIST_A - 1] = glu

        pos1 = jnp.full((rows, LANES), pos0 + 1, jnp.int32)
        u_lo = u[:, 0:LANES]
        s2 = u_lo + sp_ref[POOL_STATE - 1, :, 0:LANES]
        s4 = s2 + (sp_ref[POOL_STATE - 2, :, 0:LANES] + sp_ref[POOL_STATE - 3, :, 0:LANES])
        pooled_lo = _pool_select(s2, s4, 2, 4, pos1) - u_lo
        u_hi = u[:, LANES:2 * LANES]
        s8 = u_hi
        for i in range(1, 8):
            s8 = s8 + sp_ref[POOL_STATE - i, :, LANES:2 * LANES]
        s16 = s8
        for i in range(8, 16):
            s16 = s16 + sp_ref[POOL_STATE - i, :, LANES:2 * LANES]
        pooled_hi = _pool_select(s8, s16, 8, 16, pos1) - u_hi
        pooled = jnp.concatenate([pooled_lo, pooled_hi], axis=-1).astype(BF16)
        mixbuf[:, _M_B:_M_B + D_B] = (_dot(pooled, pool_bd[...]) * pscale[...]).astype(BF16)
        np_ref[0:POOL_STATE - 1] = sp_ref[1:POOL_STATE]
        np_ref[POOL_STATE - 1] = u

        conv_c = ccw[2:3, :] * v + ccw[1:2, :] * sc_ref[:, 1, :] + ccw[0:1, :] * sc_ref[:, 0, :]
        mixbuf[:, _M_C:_M_C + D_C] = (z[:, _O_CB:_O_CB + D_C] * conv_c).astype(BF16)
        nc_ref[:, 0, :] = sc_ref[:, 1, :]
        nc_ref[:, 1, :] = v

        h1 = h + _dot(mixbuf[...], w_out[...])
        h2buf[...] = h1
        mbuf[...] = _rms_norm(h1, g_mlp[...]).astype(BF16)

    act = jnp.square(jnp.maximum(_dot(mbuf[...], w_up[...]), 0.0)).astype(BF16)
    h2buf[...] += _dot(act, w_down[...])

    @pl.when(j == pl.num_programs(1) - 1)
    def _():
        h3 = _gated_embedding(h2buf[...], p_ref[:, 0, :], g_ple, w_pg, w_pp, g_fin, False)
        hbuf[...] = h3

        @pl.when(layer == pl.num_programs(0) - 1)
        def _():
            out_ref[:, 0, :] = _rms_norm(h3, g_fin[...])


_MLP_WEIGHT_POSITIONS = (6, 7)


def _resident():
    return pl.BlockSpec(memory_space=pltpu.VMEM)


def _layer_resident(arr, layer):
    zeros = (0,) * (arr.ndim - 1)
    return pl.BlockSpec((None,) + arr.shape[1:], lambda s: (layer,) + zeros, pipeline_mode=pl.Buffered(1))


def _prompt_layer(h, p_all, layer, weights, prev_states, *, final):
    batch, seq, _ = h.shape
    tile = min(SEQ_TILE, seq)
    assert seq % tile == 0 and tile % (ROW_CHUNK * (D_FF // FF_SLICE)) == 0 and tile >= HIST_A
    assert weights[6].shape[1:] == (D_MODEL, D_FF) and weights[7].shape[1:] == (D_FF, D_MODEL)
    tiles_per_seq = seq // tile
    n_tiles = batch * tiles_per_seq

    def front(s):
        i = jnp.minimum(s, n_tiles - 1)
        return i // tiles_per_seq, i % tiles_per_seq

    def back(s):
        i = jnp.maximum(s - 1, 0)
        return i // tiles_per_seq, i % tiles_per_seq

    def ahead(s):
        i = jnp.minimum(s + 1, n_tiles - 1)
        return i // tiles_per_seq, i % tiles_per_seq

    h_ahead = pl.BlockSpec((None, tile, D_MODEL), lambda s: (*ahead(s), 0))
    h_back = pl.BlockSpec((None, tile, D_MODEL), lambda s: (*back(s), 0))
    p_in = pl.BlockSpec((None, None, tile, D_PLE), lambda s: (layer, *back(s), 0))
    h_out = pl.BlockSpec((None, tile, D_MODEL), lambda s: (*back(s), 0))
    state_shapes = ((layer + 1, N_HIST_A, batch, D_A), (layer + 1, POOL_STATE, batch, D_B),
                    (layer + 1, batch, N_HIST_C, D_C))
    whole = lambda shape: pl.BlockSpec(shape, lambda s: (0,) * len(shape))
    assert len(prev_states) == (3 if layer else 0)
    return pl.pallas_call(
        functools.partial(_prompt_kernel, tile=tile, n_tiles=n_tiles, tiles_per_seq=tiles_per_seq,
                          layer=layer, final=final),
        grid=(n_tiles + 1,),
        in_specs=[h_ahead, h_back, p_in] + [
            pl.BlockSpec(memory_space=pl.ANY) if i in _MLP_WEIGHT_POSITIONS else _layer_resident(w, layer)
            for i, w in enumerate(weights)] + [whole(x.shape) for x in prev_states],
        out_specs=[h_out] + [whole(shape) for shape in state_shapes],
        out_shape=[jax.ShapeDtypeStruct(h.shape, F32)] + [jax.ShapeDtypeStruct(shape, F32) for shape in state_shapes],
        scratch_shapes=[pltpu.VMEM((tile, D_IN), F32),
                        pltpu.VMEM((D_A // LANES, HIST_A + tile, LANES), F32),
                        pltpu.VMEM((D_B // LANES, HIST_P + tile, LANES), F32),
                        pltpu.VMEM((D_C // LANES, HIST_C + tile, LANES), F32),
                        pltpu.VMEM((tile, D_B), BF16),
                        pltpu.VMEM((tile, D_MODEL), BF16),
                        pltpu.VMEM((tile, D_MODEL), BF16),
                        pltpu.VMEM((tile, D_MODEL), F32),
                        pltpu.VMEM((tile, D_MODEL), BF16),
                        pltpu.VMEM((2, tile, FF_SLICE), BF16),
                        pltpu.VMEM((D_MODEL, D_FF), BF16),
                        pltpu.VMEM((D_FF, D_MODEL), BF16),
                        pltpu.SemaphoreType.DMA((2,))],
        compiler_params=pltpu.CompilerParams(
            dimension_semantics=("arbitrary",),
            vmem_limit_bytes=VMEM_LIMIT_BYTES),
        name="prompt_layer",
    )(h, h, p_all, *weights, *prev_states)


def _sample_layers(h, p_all, st_a, st_p, st_c, weights):
    depth = p_all.shape[0]
    rows = h.shape[0]
    n_ff = D_FF // FF_SLICE
    vecs, w_in, caw, pool_bd, ccw, w_out, w_up, w_down, w_pg, w_pp = weights

    def per_layer(arr, **kwargs):
        zeros = (0,) * (arr.ndim - 1)
        return pl.BlockSpec((None,) + arr.shape[1:], lambda l, j: (l,) + zeros, **kwargs)

    once = dict(pipeline_mode=pl.Buffered(1))
    return pl.pallas_call(
        functools.partial(_sample_kernel, pos0=PAST_LEN),
        grid=(depth, n_ff),
        in_specs=[_resident(), per_layer(p_all), per_layer(st_a, **once), per_layer(st_p, **once),
                  per_layer(st_c, **once), per_layer(vecs), per_layer(w_in), per_layer(caw), per_layer(pool_bd),
                  per_layer(ccw), per_layer(w_out),
                  pl.BlockSpec((None, D_MODEL, FF_SLICE), lambda l, j: (l, 0, j)),
                  pl.BlockSpec((None, FF_SLICE, D_MODEL), lambda l, j: (l, j, 0)),
                  per_layer(w_pg), per_layer(w_pp)],
        out_specs=[_resident(), per_layer(st_a), per_layer(st_p), per_layer(st_c)],
        out_shape=[jax.ShapeDtypeStruct(h.shape, F32),
                   jax.ShapeDtypeStruct(st_a.shape, F32),
                   jax.ShapeDtypeStruct(st_p.shape, F32),
                   jax.ShapeDtypeStruct(st_c.shape, F32)],
        scratch_shapes=[pltpu.VMEM((rows, D_MODEL), BF16), pltpu.VMEM((rows, D_MODEL), F32),
                        pltpu.VMEM((rows, D_MODEL), F32), pltpu.VMEM((rows, D_MODEL), BF16)],
        compiler_params=pltpu.CompilerParams(
            dimension_semantics=("arbitrary", "arbitrary"),
            vmem_limit_bytes=VMEM_LIMIT_BYTES),
        name="sample_layers",
    )(h, p_all, st_a, st_p, st_c, *weights)


def _pool_block_diag(pool_w):
    depth, groups, c, _ = pool_w.shape
    eye = jnp.eye(groups, dtype=pool_w.dtype)
    return (eye[None, :, None, :, None] * pool_w[:, :, :, None, :]).reshape(depth, groups * c, groups * c)


def _history_major(x):
    return jnp.transpose(x, (0, 2, 1, 3))


def kernel(x_prompt, x_sample, state_conv_a, state_pool, state_conv_c, p_prompt, p_sample, norm_mix_g, w_in, conv_a_w, conv_a_b, ln_a_g, ln_a_b, pool_w, pool_scale, conv_c_w, w_out, norm_mlp_g, w_up, w_down, norm_ple_g, w_ple_gate, w_ple_proj, final_norm_g):
    depth = w_in.shape[0]
    dec_batch, dec_seq, _ = x_sample.shape
    assert dec_seq == 1
    hp = x_prompt
    hs = x_sample
    sa, sp = _history_major(state_conv_a), _history_major(state_pool)
    vectors = dict(g_mix=norm_mix_g, g_mlp=norm_mlp_g, g_ple=norm_ple_g,
                   g_fin=jnp.broadcast_to(final_norm_g, (depth, D_MODEL)),
                   cab=conv_a_b, lng=ln_a_g, lnb=ln_a_b, pscale=pool_scale)
    vecs = jnp.concatenate([vectors[name] for name, _ in _VEC_SIZES], axis=-1).reshape(depth, 1, _VEC_WIDTH)
    weights = (vecs, w_in.astype(BF16), conv_a_w, _pool_block_diag(pool_w).astype(BF16), conv_c_w,
               w_out.astype(BF16), w_up.astype(BF16), w_down.astype(BF16),
               w_ple_gate.astype(BF16), w_ple_proj.astype(BF16))
    prompt_states = ()
    for i in range(depth):
        hp, *prompt_states = _prompt_layer(hp, p_prompt, i, weights, prompt_states, final=i == depth - 1)
    hs, new_sa, new_sp, new_sc = _sample_layers(hs, p_sample, sa, sp, state_conv_c, weights)
    new_pa, new_pp, new_pc = prompt_states
    return (hp, hs, _history_major(new_pa), _history_major(new_pp), new_pc,
            _history_major(new_sa), _history_major(new_sp), new_sc)
```

```python
import functools

import jax
import jax.numpy as jnp
from jax import lax
from jax.experimental import pallas as pl
from jax.experimental.pallas import tpu as pltpu

D_MODEL = 1024
D_A = 384
D_B = 256
D_C = 384
D_IN = 2 * D_A + D_B + 3 * D_C
D_FF = 4096
D_PLE = 256
POOL_WINDOWS = (2, 4, 8, 16)
POOL_GROUP_DIM = D_B // len(POOL_WINDOWS)
POOL_STATE = 15
CONV_A_WIDTH = 31
CONV_C_WIDTH = 3
N_HIST_A = CONV_A_WIDTH - 1
N_HIST_C = CONV_C_WIDTH - 1
PAST_LEN = 16384
EPS = 1e-6

_O_AV = 0
_O_AG = D_A
_O_UB = 2 * D_A
_O_CB = _O_UB + D_B
_O_CC = _O_CB + D_C
_O_CX = _O_CC + D_C
_M_A = 0
_M_B = D_A
_M_C = D_A + D_B

LANES = 128
HIST_A = 32
HIST_P = 16
HIST_C = 8
ROW_CHUNK = 64
SAMPLE_ROW_CHUNK = 32
FF_SLICE = 1024
SEQ_TILE = 512
VMEM_LIMIT_BYTES = 62 * 1024 * 1024

BF16 = jnp.bfloat16
F32 = jnp.float32


_VEC_SIZES = (("g_mix", D_MODEL), ("g_mlp", D_MODEL), ("g_ple", D_MODEL), ("g_fin", D_MODEL),
              ("cab", D_A), ("lng", D_A), ("lnb", D_A), ("pscale", D_B))
_VEC_LAYOUT = {}
_VEC_WIDTH = 0
for _name, _size in _VEC_SIZES:
    _VEC_LAYOUT[_name] = (_VEC_WIDTH, _size)
    _VEC_WIDTH += _size


class _Cols:
    def __init__(self, ref, start, size):
        self.ref, self.start, self.size = ref, start, size

    def __getitem__(self, idx):
        if idx is Ellipsis:
            return self.ref[:, self.start:self.start + self.size]
        rows, cols = idx
        lo = cols.start or 0
        hi = self.size if cols.stop is None else cols.stop
        return self.ref[rows, self.start + lo:self.start + hi]


def _vector_params(vecs):
    return tuple(_Cols(vecs, *_VEC_LAYOUT[name]) for name, _ in _VEC_SIZES)


def _rms_norm(x, g):
    ms = jnp.mean(x * x, axis=-1, keepdims=True)
    return x * lax.rsqrt(ms + EPS) * g


def _dot(a, b):
    return jnp.dot(a, b, preferred_element_type=F32)


def _layer_norm_silu(ya, g, b):
    mu = jnp.mean(ya, axis=-1, keepdims=True)
    xc = ya - mu
    var = jnp.mean(xc * xc, axis=-1, keepdims=True)
    y = xc * lax.rsqrt(var + EPS) * g + b
    return y * jax.nn.sigmoid(y)


def _pool_select(s_small, s_big, w_small, w_big, pos1):
    lane = lax.broadcasted_iota(jnp.int32, s_small.shape, 1)
    cnt_small = jnp.minimum(pos1, w_small).astype(F32)
    cnt_big = jnp.minimum(pos1, w_big).astype(F32)
    return jnp.where(lane < POOL_GROUP_DIM, s_small / cnt_small, s_big / cnt_big)


def _lane_cols(offset, c):
    return slice(offset + c * LANES, offset + (c + 1) * LANES)


def _gated_embedding(h2, p, g_ple, w_pg, w_pp, g_fin, final):
    r3 = _rms_norm(h2, g_ple[...]).astype(BF16)
    gate = jax.nn.sigmoid(_dot(r3, w_pg[...]))
    h3 = h2 + gate * _dot(p.astype(BF16), w_pp[...])
    if final:
        h3 = _rms_norm(h3, g_fin[...])
    return h3


def _prompt_kernel(hn_ref, hb_ref, p_ref, vecs, w_in, caw, pool_bd, ccw, *rest,
                   tile, n_tiles, tiles_per_seq, layer, final):
    late_hbm, rest = rest[:len(_LATE_WEIGHT_POSITIONS)], rest[len(_LATE_WEIGHT_POSITIONS):]
    prev_states, rest = rest[:3 if layer else 0], rest[3 if layer else 0:]
    (out_ref, na_ref, np_ref, nc_ref,
     zbuf, gbuf, ubuf, vbuf, poolbuf, mixbuf, nbuf, h2buf, mbuf, actbuf,
     w_out, w_up, w_down, w_pg, w_pp, w_sem) = rest
    late_vmem = (w_out, w_up, w_down, w_pg, w_pp)
    g_mix, g_mlp, g_ple, g_fin, cab, lng, lnb, pscale = _vector_params(vecs)
    s = pl.program_id(0)
    t = s % tiles_per_seq

    def mlp_weight_copies():
        return [pltpu.make_async_copy(src.at[layer], dst, w_sem.at[i])
                for i, (src, dst) in enumerate(zip(late_hbm, late_vmem))]

    @pl.when(s == 0)
    def _():
        for copy in mlp_weight_copies():
            copy.start()
        for prev, new in zip(prev_states, (na_ref, np_ref, nc_ref)):
            new[0:layer] = prev[...]

    @pl.when(s == 1)
    def _():
        for copy in mlp_weight_copies():
            copy.wait()

    def shifted(buf, c, start):
        return buf[c, pl.ds(start, ROW_CHUNK, stride=1), :]

    def mixers(r0):
        rows = pl.ds(r0, ROW_CHUNK)
        for c in range(D_A // LANES):
            glu = zbuf[rows, _lane_cols(_O_AV, c)] * jax.nn.sigmoid(zbuf[rows, _lane_cols(_O_AG, c)])
            gbuf[c, pl.ds(r0 + HIST_A, ROW_CHUNK), :] = glu
        for c in range(D_B // LANES):
            ubuf[c, pl.ds(r0 + HIST_P, ROW_CHUNK), :] = zbuf[rows, _lane_cols(_O_UB, c)]
        for c in range(D_C // LANES):
            vbuf[c, pl.ds(r0 + HIST_C, ROW_CHUNK), :] = (
                zbuf[rows, _lane_cols(_O_CC, c)] * zbuf[rows, _lane_cols(_O_CX, c)])

        conv_a = []
        for c in range(D_A // LANES):
            acc = jnp.broadcast_to(cab[:, _lane_cols(0, c)], (ROW_CHUNK, LANES))
            for k in range(CONV_A_WIDTH):
                acc = acc + caw[k:k + 1, _lane_cols(0, c)] * shifted(gbuf, c, r0 + HIST_A - N_HIST_A + k)
            conv_a.append(acc)
        out_a = _layer_norm_silu(jnp.concatenate(conv_a, axis=-1), lng[...], lnb[...])
        mixbuf[rows, _M_A:_M_A + D_A] = out_a.astype(BF16)

        pos1 = (t * tile + r0 + 1) + lax.broadcasted_iota(jnp.int32, (ROW_CHUNK, LANES), 0)
        u_lo = [shifted(ubuf, 0, r0 + HIST_P - j) for j in range(4)]
        s2 = u_lo[0] + u_lo[1]
        s4 = s2 + (u_lo[2] + u_lo[3])
        pooled_lo = _pool_select(s2, s4, 2, 4, pos1) - u_lo[0]
        u_hi = [shifted(ubuf, 1, r0 + HIST_P - j) for j in range(16)]
        s8 = u_hi[0]
        for j in range(1, 8):
            s8 = s8 + u_hi[j]
        s16 = s8
        for j in range(8, 16):
            s16 = s16 + u_hi[j]
        pooled_hi = _pool_select(s8, s16, 8, 16, pos1) - u_hi[0]
        poolbuf[rows, 0:LANES] = pooled_lo.astype(BF16)
        poolbuf[rows, LANES:2 * LANES] = pooled_hi.astype(BF16)

        for c in range(D_C // LANES):
            conv_c = ccw[2:3, _lane_cols(0, c)] * shifted(vbuf, c, r0 + HIST_C)
            conv_c = conv_c + ccw[1:2, _lane_cols(0, c)] * shifted(vbuf, c, r0 + HIST_C - 1)
            conv_c = conv_c + ccw[0:1, _lane_cols(0, c)] * shifted(vbuf, c, r0 + HIST_C - 2)
            out_c = zbuf[rows, _lane_cols(_O_CB, c)] * conv_c
            mixbuf[rows, _lane_cols(_M_C, c)] = out_c.astype(BF16)

    n_ff = D_FF // FF_SLICE
    chunks_per_slice = tile // ROW_CHUNK // n_ff

    def mixer_blocks(j):
        for q in range(chunks_per_slice):
            mixers((j * chunks_per_slice + q) * ROW_CHUNK)

    def up_act(j):
        up = _dot(mbuf[...], w_up[:, j * FF_SLICE:(j + 1) * FF_SLICE])
        actbuf[j % 2] = jnp.square(jnp.maximum(up, 0.0)).astype(BF16)

    def step(front, back):
        if front:
            @pl.when(t == 0)
            def _():
                gbuf[:, 0:HIST_A, :] = jnp.zeros((D_A // LANES, HIST_A, LANES), F32)
                ubuf[:, 0:HIST_P, :] = jnp.zeros((D_B // LANES, HIST_P, LANES), F32)
                vbuf[:, 0:HIST_C, :] = jnp.zeros((D_C // LANES, HIST_C, LANES), F32)

        if front and not back:
            nbuf[...] = _rms_norm(hb_ref[...], g_mix[...]).astype(BF16)
        if back:
            h1 = hb_ref[...] + _dot(mixbuf[...], w_out[...])
        if front:
            zbuf[...] = _dot(nbuf[...], w_in[...])
            nbuf[...] = _rms_norm(hn_ref[...], g_mix[...]).astype(BF16)
        if back:
            h2buf[...] = h1
            mbuf[...] = _rms_norm(h1, g_mlp[...]).astype(BF16)
            up_act(0)

        for j in range(n_ff):
            if front:
                mixer_blocks(j)
            if back:
                down = _dot(actbuf[j % 2], w_down[j * FF_SLICE:(j + 1) * FF_SLICE, :])
                if j + 1 < n_ff:
                    up_act(j + 1)
                h2buf[...] += down

        if front:
            out_b = _dot(poolbuf[...], pool_bd[...]) * pscale[...]
            mixbuf[:, _M_B:_M_B + D_B] = out_b.astype(BF16)
            gbuf[:, 0:HIST_A, :] = gbuf[:, tile:tile + HIST_A, :]
            ubuf[:, 0:HIST_P, :] = ubuf[:, tile:tile + HIST_P, :]
            vbuf[:, 0:HIST_C, :] = vbuf[:, tile:tile + HIST_C, :]
        if back:
            out_ref[...] = _gated_embedding(h2buf[...], p_ref[...], g_ple, w_pg, w_pp, g_fin, final)

        if front:
            @pl.when(t == tiles_per_seq - 1)
            def _():
                for c in range(D_C // LANES):
                    nc_ref[layer, s // tiles_per_seq, :, _lane_cols(0, c)] = (
                        vbuf[c, HIST_C + tile - N_HIST_C:HIST_C + tile, :])

                for b in range(n_tiles // tiles_per_seq):
                    @pl.when(s == (b + 1) * tiles_per_seq - 1)
                    def _(b=b):
                        for c in range(D_A // LANES):
                            na_ref[layer, :, b, _lane_cols(0, c)] = gbuf[c, HIST_A + tile - N_HIST_A:HIST_A + tile, :]
                        for c in range(D_B // LANES):
                            np_ref[layer, :, b, _lane_cols(0, c)] = ubuf[c, HIST_P + tile - POOL_STATE:HIST_P + tile, :]

    pl.when(s == 0)(lambda: step(True, False))
    pl.when((s > 0) & (s < n_tiles))(lambda: step(True, True))
    pl.when(s == n_tiles)(lambda: step(False, True))


def _sample_kernel(h_ref, p_ref, sa_ref, sp_ref, sc_ref, vecs, w_in, caw, pool_bd, ccw, w_out, w_up, w_down, w_pg, w_pp,
                   out_ref, na_ref, np_ref, nc_ref, mixbuf, hbuf, h2buf, mbuf, *, pos0):
    g_mix, g_mlp, g_ple, g_fin, cab, lng, lnb, pscale = _vector_params(vecs)
    layer = pl.program_id(0)
    j = pl.program_id(1)

    @pl.when(j == 0)
    def _():
        @pl.when(layer == 0)
        def _():
            hbuf[...] = h_ref[:, 0, :]

        h = hbuf[...]
        rows = h.shape[0]
        z = _dot(_rms_norm(h, g_mix[...]).astype(BF16), w_in[...])
        u = z[:, _O_UB:_O_UB + D_B]
        v = z[:, _O_CC:_O_CC + D_C] * z[:, _O_CX:_O_CX + D_C]
        glu = z[:, _O_AV:_O_AV + D_A] * jax.nn.sigmoid(z[:, _O_AG:_O_AG + D_A])

        for r0 in range(0, rows, SAMPLE_ROW_CHUNK):
            blk = slice(r0, r0 + SAMPLE_ROW_CHUNK)
            acc = cab[...] + caw[N_HIST_A:N_HIST_A + 1, :] * glu[blk]
            for k in range(N_HIST_A):
                acc = acc + caw[k:k + 1, :] * sa_ref[k, blk, :]
            mixbuf[blk, _M_A:_M_A + D_A] = _layer_norm_silu(acc, lng[...], lnb[...]).astype(BF16)
        na_ref[0:N_HIST_A - 1] = sa_ref[1:N_HIST_A]
        na_ref[N_HIST_A - 1] = glu

        pos1 = jnp.full((rows, LANES), pos0 + 1, jnp.int32)
        u_lo = u[:, 0:LANES]
        s2 = u_lo + sp_ref[POOL_STATE - 1, :, 0:LANES]
        s4 = s2 + (sp_ref[POOL_STATE - 2, :, 0:LANES] + sp_ref[POOL_STATE - 3, :, 0:LANES])
        pooled_lo = _pool_select(s2, s4, 2, 4, pos1) - u_lo
        u_hi = u[:, LANES:2 * LANES]
        s8 = u_hi
        for i in range(1, 8):
            s8 = s8 + sp_ref[POOL_STATE - i, :, LANES:2 * LANES]
        s16 = s8
        for i in range(8, 16):
            s16 = s16 + sp_ref[POOL_STATE - i, :, LANES:2 * LANES]
        pooled_hi = _pool_select(s8, s16, 8, 16, pos1) - u_hi
        pooled = jnp.concatenate([pooled_lo, pooled_hi], axis=-1).astype(BF16)
        mixbuf[:, _M_B:_M_B + D_B] = (_dot(pooled, pool_bd[...]) * pscale[...]).astype(BF16)
        np_ref[0:POOL_STATE - 1] = sp_ref[1:POOL_STATE]
        np_ref[POOL_STATE - 1] = u

        conv_c = ccw[2:3, :] * v + ccw[1:2, :] * sc_ref[:, 1, :] + ccw[0:1, :] * sc_ref[:, 0, :]
        mixbuf[:, _M_C:_M_C + D_C] = (z[:, _O_CB:_O_CB + D_C] * conv_c).astype(BF16)
        nc_ref[:, 0, :] = sc_ref[:, 1, :]
        nc_ref[:, 1, :] = v

        h1 = h + _dot(mixbuf[...], w_out[...])
        h2buf[...] = h1
        mbuf[...] = _rms_norm(h1, g_mlp[...]).astype(BF16)

    act = jnp.square(jnp.maximum(_dot(mbuf[...], w_up[...]), 0.0)).astype(BF16)
    h2buf[...] += _dot(act, w_down[...])

    @pl.when(j == pl.num_programs(1) - 1)
    def _():
        h3 = _gated_embedding(h2buf[...], p_ref[:, 0, :], g_ple, w_pg, w_pp, g_fin, False)
        hbuf[...] = h3

        @pl.when(layer == pl.num_programs(0) - 1)
        def _():
            out_ref[:, 0, :] = _rms_norm(h3, g_fin[...])


_LATE_WEIGHT_POSITIONS = (5, 6, 7, 8, 9)


def _resident():
    return pl.BlockSpec(memory_space=pltpu.VMEM)


def _layer_resident(arr, layer):
    zeros = (0,) * (arr.ndim - 1)
    return pl.BlockSpec((None,) + arr.shape[1:], lambda s: (layer,) + zeros, pipeline_mode=pl.Buffered(1))


def _prompt_layer(h, p_all, layer, weights, prev_states, *, final):
    batch, seq, _ = h.shape
    tile = min(SEQ_TILE, seq)
    assert seq % tile == 0 and tile % (ROW_CHUNK * (D_FF // FF_SLICE)) == 0 and tile >= HIST_A
    early = [w for i, w in enumerate(weights) if i not in _LATE_WEIGHT_POSITIONS]
    late = [weights[i] for i in _LATE_WEIGHT_POSITIONS]
    tiles_per_seq = seq // tile
    n_tiles = batch * tiles_per_seq

    def front(s):
        i = jnp.minimum(s, n_tiles - 1)
        return i // tiles_per_seq, i % tiles_per_seq

    def back(s):
        i = jnp.maximum(s - 1, 0)
        return i // tiles_per_seq, i % tiles_per_seq

    def ahead(s):
        i = jnp.minimum(s + 1, n_tiles - 1)
        return i // tiles_per_seq, i % tiles_per_seq

    h_ahead = pl.BlockSpec((None, tile, D_MODEL), lambda s: (*ahead(s), 0))
    h_back = pl.BlockSpec((None, tile, D_MODEL), lambda s: (*back(s), 0))
    p_in = pl.BlockSpec((None, None, tile, D_PLE), lambda s: (layer, *back(s), 0))
    h_out = pl.BlockSpec((None, tile, D_MODEL), lambda s: (*back(s), 0))
    state_shapes = ((layer + 1, N_HIST_A, batch, D_A), (layer + 1, POOL_STATE, batch, D_B),
                    (layer + 1, batch, N_HIST_C, D_C))
    whole = lambda shape: pl.BlockSpec(shape, lambda s: (0,) * len(shape))
    assert len(prev_states) == (3 if layer else 0)
    return pl.pallas_call(
        functools.partial(_prompt_kernel, tile=tile, n_tiles=n_tiles, tiles_per_seq=tiles_per_seq,
                          layer=layer, final=final),
        grid=(n_tiles + 1,),
        in_specs=([h_ahead, h_back, p_in] + [_layer_resident(w, layer) for w in early]
                  + [pl.BlockSpec(memory_space=pl.ANY)] * len(late) + [whole(x.shape) for x in prev_states]),
        out_specs=[h_out] + [whole(shape) for shape in state_shapes],
        out_shape=[jax.ShapeDtypeStruct(h.shape, F32)] + [jax.ShapeDtypeStruct(shape, F32) for shape in state_shapes],
        scratch_shapes=[pltpu.VMEM((tile, D_IN), F32),
                        pltpu.VMEM((D_A // LANES, HIST_A + tile, LANES), F32),
                        pltpu.VMEM((D_B // LANES, HIST_P + tile, LANES), F32),
                        pltpu.VMEM((D_C // LANES, HIST_C + tile, LANES), F32),
                        pltpu.VMEM((tile, D_B), BF16),
                        pltpu.VMEM((tile, D_MODEL), BF16),
                        pltpu.VMEM((tile, D_MODEL), BF16),
                        pltpu.VMEM((tile, D_MODEL), F32),
                        pltpu.VMEM((tile, D_MODEL), BF16),
                        pltpu.VMEM((2, tile, FF_SLICE), BF16),
                        *[pltpu.VMEM(w.shape[1:], BF16) for w in late],
                        pltpu.SemaphoreType.DMA((len(late),))],
        compiler_params=pltpu.CompilerParams(
            dimension_semantics=("arbitrary",),
            vmem_limit_bytes=VMEM_LIMIT_BYTES),
        name="prompt_layer",
    )(h, h, p_all, *early, *late, *prev_states)


def _sample_layers(h, p_all, st_a, st_p, st_c, weights):
    depth = p_all.shape[0]
    rows = h.shape[0]
    n_ff = D_FF // FF_SLICE
    vecs, w_in, caw, pool_bd, ccw, w_out, w_up, w_down, w_pg, w_pp = weights

    def per_layer(arr, **kwargs):
        zeros = (0,) * (arr.ndim - 1)
        return pl.BlockSpec((None,) + arr.shape[1:], lambda l, j: (l,) + zeros, **kwargs)

    once = dict(pipeline_mode=pl.Buffered(1))
    return pl.pallas_call(
        functools.partial(_sample_kernel, pos0=PAST_LEN),
        grid=(depth, n_ff),
        in_specs=[_resident(), per_layer(p_all), per_layer(st_a, **once), per_layer(st_p, **once),
                  per_layer(st_c, **once), per_layer(vecs), per_layer(w_in), per_layer(caw), per_layer(pool_bd),
                  per_layer(ccw), per_layer(w_out),
                  pl.BlockSpec((None, D_MODEL, FF_SLICE), lambda l, j: (l, 0, j)),
                  pl.BlockSpec((None, FF_SLICE, D_MODEL), lambda l, j: (l, j, 0)),
                  per_layer(w_pg), per_layer(w_pp)],
        out_specs=[_resident(), per_layer(st_a), per_layer(st_p), per_layer(st_c)],
        out_shape=[jax.ShapeDtypeStruct(h.shape, F32),
                   jax.ShapeDtypeStruct(st_a.shape, F32),
                   jax.ShapeDtypeStruct(st_p.shape, F32),
                   jax.ShapeDtypeStruct(st_c.shape, F32)],
        scratch_shapes=[pltpu.VMEM((rows, D_MODEL), BF16), pltpu.VMEM((rows, D_MODEL), F32),
                        pltpu.VMEM((rows, D_MODEL), F32), pltpu.VMEM((rows, D_MODEL), BF16)],
        compiler_params=pltpu.CompilerParams(
            dimension_semantics=("arbitrary", "arbitrary"),
            vmem_limit_bytes=VMEM_LIMIT_BYTES),
        name="sample_layers",
    )(h, p_all, st_a, st_p, st_c, *weights)


def _pool_block_diag(pool_w):
    depth, groups, c, _ = pool_w.shape
    eye = jnp.eye(groups, dtype=pool_w.dtype)
    return (eye[None, :, None, :, None] * pool_w[:, :, :, None, :]).reshape(depth, groups * c, groups * c)


def _history_major(x):
    return jnp.transpose(x, (0, 2, 1, 3))


def kernel(x_prompt, x_sample, state_conv_a, state_pool, state_conv_c, p_prompt, p_sample, norm_mix_g, w_in, conv_a_w, conv_a_b, ln_a_g, ln_a_b, pool_w, pool_scale, conv_c_w, w_out, norm_mlp_g, w_up, w_down, norm_ple_g, w_ple_gate, w_ple_proj, final_norm_g):
    depth = w_in.shape[0]
    dec_batch, dec_seq, _ = x_sample.shape
    assert dec_seq == 1
    hp = x_prompt
    hs = x_sample
    sa, sp = _history_major(state_conv_a), _history_major(state_pool)
    vectors = dict(g_mix=norm_mix_g, g_mlp=norm_mlp_g, g_ple=norm_ple_g,
                   g_fin=jnp.broadcast_to(final_norm_g, (depth, D_MODEL)),
                   cab=conv_a_b, lng=ln_a_g, lnb=ln_a_b, pscale=pool_scale)
    vecs = jnp.concatenate([vectors[name] for name, _ in _VEC_SIZES], axis=-1).reshape(depth, 1, _VEC_WIDTH)
    weights = (vecs, w_in.astype(BF16), conv_a_w, _pool_block_diag(pool_w).astype(BF16), conv_c_w,
               w_out.astype(BF16), w_up.astype(BF16), w_down.astype(BF16),
               w_ple_gate.astype(BF16), w_ple_proj.astype(BF16))
    prompt_states = ()
    for i in range(depth):
        hp, *prompt_states = _prompt_layer(hp, p_prompt, i, weights, prompt_states, final=i == depth - 1)
    hs, new_sa, new_sp, new_sc = _sample_layers(hs, p_sample, sa, sp, state_conv_c, weights)
    new_pa, new_pp, new_pc = prompt_states
    return (hp, hs, _history_major(new_pa), _history_major(new_pp), new_pc,
            _history_major(new_sa), _history_major(new_sp), new_sc)
```
